```python
import math
import jax, jax.numpy as jnp
from jax import lax
import numpy as np

D_MODEL = 1024
BATCH = 8
SEQ = 2048
DEPTH = 4

HEAD_DIM = 64
HA = 8
KV_A = 2
G_A = HA // KV_A
HB = 4
WA = HA * HEAD_DIM
WB = HB * 2 * HEAD_DIM
WINDOW = 128
BLK = 128
QBLK = 128
N_BUCKETS = 32
MAX_DIST = 128
D_FF = 2816
CONV_WIDTH = 3
EPS = 1e-6
QA_W, KA_W, VA_W = WA, KV_A * HEAD_DIM, KV_A * HEAD_DIM
QB_W, KB_W, VB_W = HB * 2 * HEAD_DIM, HB * 2 * HEAD_DIM, HB * 2 * HEAD_DIM
IN_W = QA_W + KA_W + VA_W + QB_W + KB_W + VB_W

kernel_name = "hybrid_swa_diffattn_convffn_encoder"


def rms_norm(x, g):
    xf = x.astype(jnp.float32)
    y = xf * lax.rsqrt(jnp.mean(xf * xf, axis=-1, keepdims=True) + EPS)
    return (y * g.astype(jnp.float32)).astype(x.dtype)


def t5_bucket(rel):
    half = N_BUCKETS // 2
    max_exact = half // 2
    ret = jnp.where(rel > 0, half, 0)
    n = jnp.abs(rel)
    nf = jnp.maximum(n, 1).astype(jnp.float32)
    large = max_exact + (jnp.log(nf / max_exact) / math.log(MAX_DIST / max_exact)
                         * (half - max_exact)).astype(jnp.int32)
    large = jnp.minimum(large, half - 1)
    return ret + jnp.where(n < max_exact, n, large)


def windowed_gqa(q, k, v, sink, bias_tab):
    B, S = q.shape[0], q.shape[1]
    nb = S // BLK
    qb = q.reshape(B, nb, BLK, KV_A, G_A, HEAD_DIM)

    def band(t):
        tp = jnp.pad(t, ((0, 0), (BLK, BLK), (0, 0), (0, 0))).reshape(B, nb + 2, BLK, KV_A, HEAD_DIM)
        return jnp.concatenate([tp[:, :-2], tp[:, 1:-1], tp[:, 2:]], axis=2)

    kw, vw = band(k), band(v)
    s = jnp.einsum('bnqkgd,bnskd->bnkgqs', qb, kw).astype(jnp.float32) * (HEAD_DIM ** -0.5)
    rel = jnp.arange(3 * BLK)[None, :] - BLK - jnp.arange(BLK)[:, None]
    kpos = (jnp.arange(nb)[:, None] - 1) * BLK + jnp.arange(3 * BLK)[None, :]
    valid = (jnp.abs(rel) <= WINDOW)[None] & ((kpos >= 0) & (kpos < S))[:, None, :]
    bias = bias_tab[t5_bucket(rel)].astype(jnp.float32).transpose(2, 0, 1)
    s = s + bias.reshape(KV_A, G_A, BLK, 3 * BLK)
    s = jnp.where(valid[None, :, None, None], s, -jnp.inf)
    sink_col = jnp.broadcast_to(sink.astype(jnp.float32).reshape(1, 1, KV_A, G_A, 1, 1),
                                s.shape[:-1] + (1,))
    p = jax.nn.softmax(jnp.concatenate([s, sink_col], axis=-1), axis=-1)[..., :-1]
    o = jnp.einsum('bnkgqs,bnskd->bnqkgd', p.astype(v.dtype), vw)
    return o.reshape(B, S, WA)


def diff_attention(q, k, v, lam, lam_init, subln_g, bias_tab):
    B, S = q.shape[0], q.shape[1]
    nb = S // QBLK
    qblocks = q.reshape(B, nb, QBLK, HB, 2, HEAD_DIM).transpose(1, 0, 2, 3, 4, 5)
    kpos = jnp.arange(S)

    def one_block(args):
        qb, i = args
        qpos = i * QBLK + jnp.arange(QBLK)
        bias = bias_tab[t5_bucket(kpos[None, :] - qpos[:, None])].astype(jnp.float32)
        bias = bias.transpose(2, 0, 1)
        s = jnp.einsum('bqhcd,bkhcd->bhcqk', qb, k).astype(jnp.float32) * (HEAD_DIM ** -0.5)
        p = jax.nn.softmax(s + bias[None, :, None], axis=-1)
        a = p[:, :, 0] - lam * p[:, :, 1]
        return jnp.einsum('bhqk,bkhe->bqhe', a.astype(v.dtype), v)

    o = lax.map(one_block, (qblocks, jnp.arange(nb)))
    o = o.transpose(1, 0, 2, 3, 4).reshape(B, S, HB, 2 * HEAD_DIM)
    o = rms_norm(o, subln_g) * (1.0 - lam_init)
    return o.reshape(B, S, WB)


def dwconv_centred(u, w, b):
    S = u.shape[1]
    pad = CONV_WIDTH // 2
    up = jnp.pad(u, ((0, 0), (pad, pad), (0, 0)))
    out = b
    for j in range(CONV_WIDTH):
        out = out + up[:, j:j + S] * w[j]
    return out


def setup_inputs(seed: int = 0) -> dict:
    key = jax.random.key(seed)
    ks = jax.random.split(key, 24)
    f32 = jnp.float32
    nrm = lambda k, shape, s: jax.random.normal(k, shape, f32) * s
    L, D = DEPTH, D_MODEL
    return {
        "x": nrm(ks[0], (BATCH, SEQ, D), 1.0),
        "ln1_g": 1.0 + nrm(ks[1], (L, D), 0.02),
        "w_in": nrm(ks[2], (L, D, IN_W), D ** -0.5),
        "qn_a": 1.0 + nrm(ks[3], (L, HEAD_DIM), 0.02),
        "kn_a": 1.0 + nrm(ks[4], (L, HEAD_DIM), 0.02),
        "sink": nrm(ks[5], (L, HA), 0.5),
        "qn_b": 1.0 + nrm(ks[6], (L, HEAD_DIM), 0.02),
        "kn_b": 1.0 + nrm(ks[7], (L, HEAD_DIM), 0.02),
        "lam_q1": nrm(ks[8], (L, HEAD_DIM), 0.1),
        "lam_k1": nrm(ks[9], (L, HEAD_DIM), 0.1),
        "lam_q2": nrm(ks[10], (L, HEAD_DIM), 0.1),
        "lam_k2": nrm(ks[11], (L, HEAD_DIM), 0.1),
        "subln_g": 1.0 + nrm(ks[12], (L, 2 * HEAD_DIM), 0.02),
        "rel_bias": nrm(ks[13], (N_BUCKETS, HA + HB), 0.5),
        "w_gate": nrm(ks[14], (L, D, 2 * D), D ** -0.5),
        "b_gate": nrm(ks[15], (L, 2 * D), 0.02),
        "w_a_proj": nrm(ks[16], (L, WA, D), WA ** -0.5),
        "w_b_proj": nrm(ks[17], (L, WB, D), WB ** -0.5),
        "w_o": nrm(ks[18], (L, D, D), D ** -0.5),
        "ln2_g": 1.0 + nrm(ks[19], (L, D), 0.02),
        "w_up": nrm(ks[20], (L, D, 2 * D_FF), D ** -0.5),
        "conv_w": nrm(ks[21], (L, CONV_WIDTH, 2 * D_FF), CONV_WIDTH ** -0.5),
        "conv_b": nrm(ks[22], (L, 2 * D_FF), 0.02),
        "w_down": nrm(ks[23], (L, D_FF, D), D_FF ** -0.5),
    }


def reference(x, ln1_g, w_in, qn_a, kn_a, sink, qn_b, kn_b, lam_q1, lam_k1, lam_q2, lam_k2,
              subln_g, rel_bias, w_gate, b_gate, w_a_proj, w_b_proj, w_o, ln2_g, w_up,
              conv_w, conv_b, w_down):
    B, S, D = x.shape
    bias_a, bias_b = rel_bias[:, :HA], rel_bias[:, HA:]
    cuts = np.cumsum([QA_W, KA_W, VA_W, QB_W, KB_W]).tolist()
    for l in range(DEPTH):
        h = rms_norm(x, ln1_g[l])
        z = h @ w_in[l]
        zqa, zka, zva, zqb, zkb, zvb = jnp.split(z, cuts, axis=-1)
        q_a = rms_norm(zqa.reshape(B, S, HA, HEAD_DIM), qn_a[l])
        k_a = rms_norm(zka.reshape(B, S, KV_A, HEAD_DIM), kn_a[l])
        v_a = zva.reshape(B, S, KV_A, HEAD_DIM)
        o_a = windowed_gqa(q_a, k_a, v_a, sink[l], bias_a)
        q_b = rms_norm(zqb.reshape(B, S, HB, 2, HEAD_DIM), qn_b[l])
        k_b = rms_norm(zkb.reshape(B, S, HB, 2, HEAD_DIM), kn_b[l])
        v_b = zvb.reshape(B, S, HB, 2 * HEAD_DIM)
        lam_init = 0.8 - 0.6 * math.exp(-0.3 * l)
        lam = (jnp.exp(jnp.sum(lam_q1[l].astype(jnp.float32) * lam_k1[l].astype(jnp.float32)))
               - jnp.exp(jnp.sum(lam_q2[l].astype(jnp.float32) * lam_k2[l].astype(jnp.float32)))
               + lam_init)
        o_b = diff_attention(q_b, k_b, v_b, lam, lam_init, subln_g[l], bias_b)
        gates = jax.nn.sigmoid(h @ w_gate[l] + b_gate[l])
        g_a, g_b = gates[..., :D], gates[..., D:]
        mix = g_a * (o_a @ w_a_proj[l]) + g_b * (o_b @ w_b_proj[l])
        x = x + mix @ w_o[l]
        h2 = rms_norm(x, ln2_g[l])
        u = dwconv_centred(h2 @ w_up[l], conv_w[l], conv_b[l])
        val, gate = u[..., :D_FF], u[..., D_FF:]
        x = x + (jax.nn.silu(gate) * val) @ w_down[l]
    return x
```

```python
import functools
import math

import numpy as np
import jax
import jax.numpy as jnp
from jax import lax
from jax.experimental import pallas as pl
from jax.experimental.pallas import tpu as pltpu

D_MODEL = 1024
BATCH = 8
SEQ = 2048
DEPTH = 4
HEAD_DIM = 64
HA = 8
KV_A = 2
HB = 4
WA = HA * HEAD_DIM
WB = HB * 2 * HEAD_DIM
WINDOW = 128
BLK = 128
N_BUCKETS = 32
MAX_DIST = 128
D_FF = 2816
CONV_WIDTH = 3
EPS = 1e-6
IN_W = WA + 2 * KV_A * HEAD_DIM + 3 * WB
SCALE = HEAD_DIM ** -0.5
NEG = -1e30

LANES = 128
SUBLANES = 8
VMEM_LIMIT = 48 * 1024 * 1024

TM_PROJ = 512
TM_FFN = 512
TC_FFN = 256
TQ_B = 256

F32 = jnp.float32
BF16 = jnp.bfloat16


def _dot(a, b):
    return jnp.dot(a, b, preferred_element_type=F32)


def _dot_nt(a, b):
    return lax.dot_general(a, b, (((1,), (1,)), ((), ())), preferred_element_type=F32)


def _rms(x, g):
    return x * lax.rsqrt(jnp.mean(x * x, axis=-1, keepdims=True) + EPS) * g


def _half_rms(x, g2, lo):
    x2 = x * x
    s_all = jnp.sum(x2, axis=-1, keepdims=True)
    s_lo = jnp.sum(jnp.where(lo, x2, 0.0), axis=-1, keepdims=True)
    ss = jnp.where(lo, s_lo, s_all - s_lo)
    return x * lax.rsqrt(ss * (1.0 / HEAD_DIM) + EPS) * g2


def _lo_mask():
    return lax.broadcasted_iota(jnp.int32, (1, LANES), 1) < HEAD_DIM


def _inproj_kernel(x_ref, g_ref, w_ref, z_ref):
    h = _rms(x_ref[...], g_ref[...]).astype(BF16)
    z_ref[...] = _dot(h, w_ref[...])


def _inproj(xf, ln_g, w_in, layer):
    n_tok = xf.shape[0]
    return pl.pallas_call(
        _inproj_kernel,
        grid=(n_tok // TM_PROJ,),
        in_specs=[
            pl.BlockSpec((TM_PROJ, D_MODEL), lambda i: (i, 0)),
            pl.BlockSpec((None, 1, D_MODEL), lambda i: (layer, 0, 0)),
            pl.BlockSpec((None, D_MODEL, IN_W), lambda i: (layer, 0, 0)),
        ],
        out_specs=pl.BlockSpec((TM_PROJ, IN_W), lambda i: (i, 0)),
        out_shape=jax.ShapeDtypeStruct((n_tok, IN_W), F32),
        compiler_params=pltpu.CompilerParams(
            dimension_semantics=("parallel",), vmem_limit_bytes=VMEM_LIMIT),
        name=f"inproj_l{layer}",
    )(xf, ln_g, w_in)


def _mixa_kernel(sink_ref, q_ref, kv_ref, qg_ref, kg_ref, bias_ref, o_ref, kk_ref, vv_ref, *, layer):
    nb = SEQ // BLK
    lo = _lo_mask()
    row_lo = lax.broadcasted_iota(jnp.int32, (2 * BLK, 1), 0) < BLK

    kv = kv_ref[...]
    k = _half_rms(kv[:, :LANES], kg_ref[...], lo)
    v = kv[:, LANES:]
    kr = pltpu.roll(k, HEAD_DIM, 1)
    vr = pltpu.roll(v, HEAD_DIM, 1)
    zeros = jnp.zeros((BLK, LANES), BF16)
    for j in range(KV_A):
        for ref in (kk_ref, vv_ref):
            ref[j, 0:BLK, :] = zeros
            ref[j, SEQ + BLK:SEQ + 2 * BLK, :] = zeros
    kk_ref[0, BLK:BLK + SEQ, :] = jnp.where(lo, k, kr).astype(BF16)
    kk_ref[1, BLK:BLK + SEQ, :] = jnp.where(lo, kr, k).astype(BF16)
    vv_ref[0, BLK:BLK + SEQ, :] = jnp.where(lo, v, vr).astype(BF16)
    vv_ref[1, BLK:BLK + SEQ, :] = jnp.where(lo, vr, v).astype(BF16)

    def body(n, carry):
        r0 = pl.multiple_of(n * BLK, BLK)
        variant = jnp.where(n == 0, 0, jnp.where(n == nb - 1, 2, 1))
        qblk = q_ref[pl.ds(r0, BLK), :]
        for m in range(HA // 2):
            j = m // 2
            qn = _half_rms(qblk[:, LANES * m:LANES * (m + 1)], qg_ref[...], lo) * SCALE
            qs = jnp.concatenate(
                [jnp.where(lo, qn, 0.0), jnp.where(lo, 0.0, qn)], axis=0).astype(BF16)
            s = _dot_nt(qs, kk_ref[j, pl.ds(r0, 3 * BLK), :]) + bias_ref[variant, m]
            snk = jnp.where(row_lo, sink_ref[layer, 2 * m], sink_ref[layer, 2 * m + 1])
            mx = jnp.maximum(jnp.max(s, axis=-1, keepdims=True), snk)
            p = jnp.exp(s - mx)
            denom = jnp.sum(p, axis=-1, keepdims=True) + jnp.exp(snk - mx)
            o = _dot(p.astype(BF16), vv_ref[j, pl.ds(r0, 3 * BLK), :]) * (1.0 / denom)
            o_ref[pl.ds(r0, BLK), LANES * m:LANES * (m + 1)] = jnp.where(
                lo, o[:BLK], o[BLK:]).astype(o_ref.dtype)
        return carry

    lax.fori_loop(0, nb, body, 0)


def _mixa(z3, sink, qg2, kg2, tab_a, layer):
    bsz = z3.shape[0]
    return pl.pallas_call(
        functools.partial(_mixa_kernel, layer=layer),
        grid=(bsz,),
        in_specs=[
            pl.BlockSpec(memory_space=pltpu.SMEM),
            pl.BlockSpec((None, SEQ, WA), lambda b: (b, 0, 0)),
            pl.BlockSpec((None, SEQ, 2 * LANES), lambda b: (b, 0, WA // (2 * LANES))),
            pl.BlockSpec((None, 1, LANES), lambda b: (layer, 0, 0)),
            pl.BlockSpec((None, 1, LANES), lambda b: (layer, 0, 0)),
            pl.BlockSpec((3, HA // 2, 2 * BLK, 3 * BLK), lambda b: (0, 0, 0, 0)),
        ],
        out_specs=pl.BlockSpec((None, SEQ, WA), lambda b: (b, 0, 0)),
        out_shape=jax.ShapeDtypeStruct((bsz, SEQ, WA), BF16),
        scratch_shapes=[
            pltpu.VMEM((KV_A, SEQ + 2 * BLK, LANES), BF16),
            pltpu.VMEM((KV_A, SEQ + 2 * BLK, LANES), BF16),
        ],
        compiler_params=pltpu.CompilerParams(
            dimension_semantics=("parallel",), vmem_limit_bytes=VMEM_LIMIT),
        name=f"mixa_l{layer}",
    )(sink, z3, z3, qg2, kg2, tab_a)


def _mixb_kernel(q_ref, k_ref, v_ref, qg_ref, kg_ref, lam_ref, sg_ref, w_ref, o_ref,
                 q0_ref, q1_ref, kn_ref, vb_ref, *, lam_init):
    lo = _lo_mask()
    kn_ref[...] = _half_rms(k_ref[...], kg_ref[...], lo).astype(BF16)
    vb_ref[...] = v_ref[...].astype(BF16)
    qn = _half_rms(q_ref[...], qg_ref[...], lo) * SCALE
    q0_ref[...] = jnp.where(lo, qn, 0.0).astype(BF16)
    q1_ref[...] = jnp.where(lo, 0.0, qn).astype(BF16)
    lv = lam_ref[...]
    lam = (jnp.exp(jnp.sum(lv[0:1] * lv[1:2], axis=-1, keepdims=True))
           - jnp.exp(jnp.sum(lv[2:3] * lv[3:4], axis=-1, keepdims=True)) + lam_init)

    def body(i, carry):
        r0 = pl.multiple_of(i * TQ_B, TQ_B)
        off = pl.multiple_of(SEQ - i * TQ_B, TQ_B)
        bias = w_ref[:, pl.ds(off, SEQ)]
        kn = kn_ref[...]
        s0 = _dot_nt(q0_ref[pl.ds(r0, TQ_B), :], kn) + bias
        s1 = _dot_nt(q1_ref[pl.ds(r0, TQ_B), :], kn) + bias
        p0 = jnp.exp(s0 - jnp.max(s0, axis=-1, keepdims=True))
        p1 = jnp.exp(s1 - jnp.max(s1, axis=-1, keepdims=True))
        c0 = 1.0 / jnp.sum(p0, axis=-1, keepdims=True)
        c1 = lam / jnp.sum(p1, axis=-1, keepdims=True)
        a = p0 * c0 - p1 * c1
        o = _dot(a.astype(BF16), vb_ref[...])
        o = _rms(o, sg_ref[...]) * (1.0 - lam_init)
        o_ref[pl.ds(r0, TQ_B), :] = o.astype(o_ref.dtype)
        return carry

    lax.fori_loop(0, SEQ // TQ_B, body, 0)


def _mixb(z3, qg2, kg2, lam_vecs, subln_g, tab_b, layer):
    bsz = z3.shape[0]
    lam_init = 0.8 - 0.6 * math.exp(-0.3 * layer)
    qb0 = (WA + 2 * KV_A * HEAD_DIM) // LANES
    kb0 = qb0 + WB // LANES
    vb0 = kb0 + WB // LANES
    return pl.pallas_call(
        functools.partial(_mixb_kernel, lam_init=lam_init),
        grid=(bsz, HB),
        in_specs=[
            pl.BlockSpec((None, SEQ, LANES), lambda b, h: (b, 0, qb0 + h)),
            pl.BlockSpec((None, SEQ, LANES), lambda b, h: (b, 0, kb0 + h)),
            pl.BlockSpec((None, SEQ, LANES), lambda b, h: (b, 0, vb0 + h)),
            pl.BlockSpec((None, 1, LANES), lambda b, h: (layer, 0, 0)),
            pl.BlockSpec((None, 1, LANES), lambda b, h: (layer, 0, 0)),
            pl.BlockSpec((None, 4, HEAD_DIM), lambda b, h: (layer, 0, 0)),
            pl.BlockSpec((None, 1, LANES), lambda b, h: (layer, 0, 0)),
            pl.BlockSpec((None, TQ_B, 2 * SEQ), lambda b, h: (h, 0, 0)),
        ],
        out_specs=pl.BlockSpec((None, SEQ, LANES), lambda b, h: (b, 0, h)),
        out_shape=jax.ShapeDtypeStruct((bsz, SEQ, WB), BF16),
        scratch_shapes=[pltpu.VMEM((SEQ, LANES), BF16) for _ in range(4)],
        compiler_params=pltpu.CompilerParams(
            dimension_semantics=("parallel", "arbitrary"), vmem_limit_bytes=VMEM_LIMIT),
        name=f"mixb_l{layer}",
    )(z3, z3, z3, qg2, kg2, lam_vecs, subln_g, tab_b)


def _merge_kernel(x_ref, oa_ref, ob_ref, g_ref, wg_ref, bg_ref, wa_ref, wb_ref, wo_ref, o_ref):
    x = x_ref[...]
    h = _rms(x, g_ref[...]).astype(BF16)
    gates = 1.0 / (1.0 + jnp.exp(-(_dot(h, wg_ref[...]) + bg_ref[...])))
    mix = (gates[:, :D_MODEL] * _dot(oa_ref[...], wa_ref[...])
           + gates[:, D_MODEL:] * _dot(ob_ref[...], wb_ref[...]))
    o_ref[...] = x + _dot(mix.astype(BF16), wo_ref[...])


def _merge(xf, oa, ob, ln_g, w_gate, b_gate, w_a, w_b, w_o, layer):
    n_tok = xf.shape[0]
    lsel = lambda i: (layer, 0, 0)
    return pl.pallas_call(
        _merge_kernel,
        grid=(n_tok // TM_PROJ,),
        in_specs=[
            pl.BlockSpec((TM_PROJ, D_MODEL), lambda i: (i, 0)),
            pl.BlockSpec((TM_PROJ, WA), lambda i: (i, 0)),
            pl.BlockSpec((TM_PROJ, WB), lambda i: (i, 0)),
            pl.BlockSpec((None, 1, D_MODEL), lsel),
            pl.BlockSpec((None, D_MODEL, 2 * D_MODEL), lsel),
            pl.BlockSpec((None, 1, 2 * D_MODEL), lsel),
            pl.BlockSpec((None, WA, D_MODEL), lsel),
            pl.BlockSpec((None, WB, D_MODEL), lsel),
            pl.BlockSpec((None, D_MODEL, D_MODEL), lsel),
        ],
        out_specs=pl.BlockSpec((TM_PROJ, D_MODEL), lambda i: (i, 0)),
        out_shape=jax.ShapeDtypeStruct((n_tok, D_MODEL), F32),
        compiler_params=pltpu.CompilerParams(
            dimension_semantics=("parallel",), vmem_limit_bytes=VMEM_LIMIT),
        name=f"merge_l{layer}",
    )(xf, oa, ob, ln_g, w_gate, b_gate, w_a, w_b, w_o)


def _ffn_kernel(xp_ref, xc_ref, xn_ref, g_ref, wup_ref, cw_ref, cb_ref, wdn_ref, o_ref,
                us_ref, act_ref):
    tiles_per_seq = SEQ // TM_FFN
    t = pl.program_id(0) % tiles_per_seq
    g = g_ref[...]
    xc = xc_ref[...]
    hp = jnp.where(t == 0, 0.0, _rms(xp_ref[...], g))
    hn = jnp.where(t == tiles_per_seq - 1, 0.0, _rms(xn_ref[...], g))
    h2 = jnp.concatenate([hp, _rms(xc, g), hn], axis=0).astype(BF16)

    def conv(part, col):
        w = cw_ref[:, col:col + TC_FFN]
        acc = cb_ref[:, col:col + TC_FFN]
        for j in range(CONV_WIDTH):
            acc = acc + us_ref[part, SUBLANES - 1 + j:SUBLANES - 1 + j + TM_FFN, :] * w[j:j + 1]
        return acc

    for c in range(D_FF // TC_FFN):
        cv, cg = c * TC_FFN, D_FF + c * TC_FFN
        us_ref[0] = _dot(h2, wup_ref[:, cv:cv + TC_FFN])
        us_ref[1] = _dot(h2, wup_ref[:, cg:cg + TC_FFN])
        val = conv(0, cv)
        gate = conv(1, cg)
        act = gate * (1.0 / (1.0 + jnp.exp(-gate))) * val
        act_ref[:, cv:cv + TC_FFN] = act.astype(BF16)
    o_ref[...] = xc + _dot(act_ref[...], wdn_ref[...])


def _ffn(xf, ln_g, w_up, conv_w, conv_b, w_down, layer):
    n_tok = xf.shape[0]
    lsel = lambda i: (layer, 0, 0)
    halo_per_tile = TM_FFN // SUBLANES
    n_halo = n_tok // SUBLANES
    return pl.pallas_call(
        _ffn_kernel,
        grid=(n_tok // TM_FFN,),
        in_specs=[
            pl.BlockSpec((SUBLANES, D_MODEL), lambda i: (jnp.maximum(i * halo_per_tile - 1, 0), 0)),
            pl.BlockSpec((TM_FFN, D_MODEL), lambda i: (i, 0)),
            pl.BlockSpec((SUBLANES, D_MODEL),
                         lambda i: (jnp.minimum((i + 1) * halo_per_tile, n_halo - 1), 0)),
            pl.BlockSpec((None, 1, D_MODEL), lsel),
            pl.BlockSpec((None, D_MODEL, 2 * D_FF), lsel),
            pl.BlockSpec((None, CONV_WIDTH, 2 * D_FF), lsel),
            pl.BlockSpec((None, 1, 2 * D_FF), lsel),
            pl.BlockSpec((None, D_FF, D_MODEL), lsel),
        ],
        out_specs=pl.BlockSpec((TM_FFN, D_MODEL), lambda i: (i, 0)),
        out_shape=jax.ShapeDtypeStruct((n_tok, D_MODEL), F32),
        scratch_shapes=[
            pltpu.VMEM((2, TM_FFN + 2 * SUBLANES, TC_FFN), F32),
            pltpu.VMEM((TM_FFN, D_FF), BF16),
        ],
        compiler_params=pltpu.CompilerParams(
            dimension_semantics=("parallel",), vmem_limit_bytes=VMEM_LIMIT),
        name=f"ffn_l{layer}",
    )(xf, xf, xf, ln_g, w_up, conv_w, conv_b, w_down)


def _t5_bucket(rel):
    half = N_BUCKETS // 2
    max_exact = half // 2
    ret = jnp.where(rel > 0, half, 0)
    n = jnp.abs(rel)
    nf = jnp.maximum(n, 1).astype(jnp.float32)
    large = max_exact + (jnp.log(nf / max_exact) / math.log(MAX_DIST / max_exact)
                         * (half - max_exact)).astype(jnp.int32)
    large = jnp.minimum(large, half - 1)
    return ret + jnp.where(n < max_exact, n, large)


def _expand_bias(bucket, tab):
    out = jnp.zeros((tab.shape[1],) + bucket.shape, F32)
    for b in range(N_BUCKETS):
        out = out + jnp.where(bucket[None] == b, tab[b][:, None, None], 0.0)
    return out


def _bias_tables(rel_bias):
    bias_a, bias_b = rel_bias[:, :HA], rel_bias[:, HA:]
    rel_np = np.arange(3 * BLK)[None, :] - BLK - np.arange(BLK)[:, None]
    base = _expand_bias(_t5_bucket(jnp.asarray(rel_np, jnp.int32)), bias_a)
    col = np.arange(3 * BLK)[None, :]
    in_win = np.abs(rel_np) <= WINDOW
    variants = []
    for valid_cols in (col >= BLK, col >= 0, col < 2 * BLK):
        variants.append(jnp.where(jnp.asarray(in_win & valid_cols)[None], base, NEG))
    tab_a = jnp.stack(variants).reshape(3, HA // 2, 2 * BLK, 3 * BLK)
    rel_b = np.arange(2 * SEQ)[None, :] - SEQ - np.arange(TQ_B)[:, None]
    tab_b = _expand_bias(_t5_bucket(jnp.asarray(rel_b, jnp.int32)), bias_b)
    return tab_a, tab_b


def kernel(x, ln1_g, w_in, qn_a, kn_a, sink, qn_b, kn_b, lam_q1, lam_k1, lam_q2, lam_k2,
           subln_g, rel_bias, w_gate, b_gate, w_a_proj, w_b_proj, w_o, ln2_g, w_up,
           conv_w, conv_b, w_down):
    bsz, seq, d = x.shape
    assert (bsz, seq, d) == (BATCH, SEQ, D_MODEL)
    tab_a, tab_b = _bias_tables(rel_bias.astype(F32))
    row = lambda a: a.astype(F32)[:, None, :]
    twice = lambda a: jnp.concatenate([a, a], axis=-1).astype(F32)[:, None, :]
    ln1, ln2, sg = row(ln1_g), row(ln2_g), row(subln_g)
    qga, kga, qgb, kgb = twice(qn_a), twice(kn_a), twice(qn_b), twice(kn_b)
    lam_vecs = jnp.stack([lam_q1, lam_k1, lam_q2, lam_k2], axis=1).astype(F32)
    bg, cb = row(b_gate), row(conv_b)
    cw = conv_w.astype(F32)
    snk = sink.astype(F32)
    w_in_b, w_gate_b, w_a_b, w_b_b, w_o_b, w_up_b, w_down_b = (
        w.astype(BF16) for w in (w_in, w_gate, w_a_proj, w_b_proj, w_o, w_up, w_down))

    xf = x.astype(F32).reshape(bsz * seq, d)
    for layer in range(DEPTH):
        z3 = _inproj(xf, ln1, w_in_b, layer).reshape(bsz, seq, IN_W)
        oa = _mixa(z3, snk, qga, kga, tab_a, layer).reshape(bsz * seq, WA)
        ob = _mixb(z3, qgb, kgb, lam_vecs, sg, tab_b, layer).reshape(bsz * seq, WB)
        xf = _merge(xf, oa, ob, ln1, w_gate_b, bg, w_a_b, w_b_b, w_o_b, layer)
        xf = _ffn(xf, ln2, w_up_b, cw, cb, w_down_b, layer)
    return xf.reshape(bsz, seq, d).astype(x.dtype)
```

```python
import functools
import math

import numpy as np
import jax
import jax.numpy as jnp
from jax import lax
from jax.experimental import pallas as pl
from jax.experimental.pallas import tpu as pltpu

D_MODEL = 1024
BATCH = 8
SEQ = 2048
DEPTH = 4
HEAD_DIM = 64
HA = 8
KV_A = 2
HB = 4
WA = HA * HEAD_DIM
WB = HB * 2 * HEAD_DIM
WINDOW = 128
BLK = 128
N_BUCKETS = 32
MAX_DIST = 128
D_FF = 2816
CONV_WIDTH = 3
EPS = 1e-6
IN_W = WA + 2 * KV_A * HEAD_DIM + 3 * WB
SCALE = HEAD_DIM ** -0.5
NEG = -1e30

LANES = 128
SUBLANES = 8
VMEM_LIMIT = 48 * 1024 * 1024

TM_PROJ = 512
TM_FFN = 512
TC_FFN = 256
TQ_B = 512
NKB_B = SEQ // BLK
NBAND_B = TQ_B // BLK + 2
LOG2E = math.log2(math.e)
assert 2 * NKB_B <= LANES and SEQ // TQ_B >= 3 and MAX_DIST <= BLK

F32 = jnp.float32
BF16 = jnp.bfloat16


def _dot(a, b):
    return jnp.dot(a, b, preferred_element_type=F32)


def _dot_nt(a, b):
    return lax.dot_general(a, b, (((1,), (1,)), ((), ())), preferred_element_type=F32)


def _rms(x, g):
    return x * lax.rsqrt(jnp.mean(x * x, axis=-1, keepdims=True) + EPS) * g


def _half_rms(x, g2, lo):
    x2 = x * x
    s_all = jnp.sum(x2, axis=-1, keepdims=True)
    s_lo = jnp.sum(jnp.where(lo, x2, 0.0), axis=-1, keepdims=True)
    ss = jnp.where(lo, s_lo, s_all - s_lo)
    return x * lax.rsqrt(ss * (1.0 / HEAD_DIM) + EPS) * g2


def _lo_mask():
    return lax.broadcasted_iota(jnp.int32, (1, LANES), 1) < HEAD_DIM


def _inproj_kernel(x_ref, g_ref, w_ref, z_ref):
    h = _rms(x_ref[...], g_ref[...]).astype(BF16)
    z_ref[...] = _dot(h, w_ref[...])


def _inproj(xf, ln_g, w_in, layer):
    n_tok = xf.shape[0]
    return pl.pallas_call(
        _inproj_kernel,
        grid=(n_tok // TM_PROJ,),
        in_specs=[
            pl.BlockSpec((TM_PROJ, D_MODEL), lambda i: (i, 0)),
            pl.BlockSpec((None, 1, D_MODEL), lambda i: (layer, 0, 0)),
            pl.BlockSpec((None, D_MODEL, IN_W), lambda i: (layer, 0, 0)),
        ],
        out_specs=pl.BlockSpec((TM_PROJ, IN_W), lambda i: (i, 0)),
        out_shape=jax.ShapeDtypeStruct((n_tok, IN_W), F32),
        compiler_params=pltpu.CompilerParams(
            dimension_semantics=("parallel",), vmem_limit_bytes=VMEM_LIMIT),
        name=f"inproj_l{layer}",
    )(xf, ln_g, w_in)


def _mixa_kernel(sink_ref, q_ref, kv_ref, qg_ref, kg_ref, bias_ref, o_ref,
                 qlo_ref, qhi_ref, kk_ref, vv_ref, *, layer):
    nb = SEQ // BLK
    group = HA // KV_A
    lo = _lo_mask()

    kv = kv_ref[...]
    k = _half_rms(kv[:, :LANES], kg_ref[...], lo)
    v = kv[:, LANES:]
    kr = pltpu.roll(k, HEAD_DIM, 1)
    vr = pltpu.roll(v, HEAD_DIM, 1)
    for j in range(KV_A):
        for edge in (slice(0, BLK), slice(SEQ + BLK, SEQ + 2 * BLK)):
            kk_ref[j, edge, :] = jnp.zeros((BLK, LANES), BF16)
            vv_ref[j, edge, :] = jnp.zeros((BLK, 2 * LANES), BF16)
        vv_ref[j, BLK:BLK + SEQ, LANES:] = jnp.ones((SEQ, LANES), BF16)
    kk_ref[0, BLK:BLK + SEQ, :] = jnp.where(lo, k, kr).astype(BF16)
    kk_ref[1, BLK:BLK + SEQ, :] = jnp.where(lo, kr, k).astype(BF16)
    vv_ref[0, BLK:BLK + SEQ, :LANES] = jnp.where(lo, v, vr).astype(BF16)
    vv_ref[1, BLK:BLK + SEQ, :LANES] = jnp.where(lo, vr, v).astype(BF16)
    for m in range(HA // 2):
        col = slice(LANES * m, LANES * (m + 1))
        qn = _half_rms(q_ref[:, col], qg_ref[...], lo) * (SCALE * LOG2E)
        qlo_ref[:, col] = jnp.where(lo, qn, 0.0).astype(BF16)
        qhi_ref[:, col] = jnp.where(lo, 0.0, qn).astype(BF16)

    head_row = lax.broadcasted_iota(jnp.int32, (group * BLK, 1), 0) // BLK
    sinks = []
    for j in range(KV_A):
        snk = jnp.zeros((group * BLK, 1), F32)
        for g in range(group):
            snk = jnp.where(head_row == g, sink_ref[layer, group * j + g] * LOG2E, snk)
        sinks.append(snk)

    def body(n, carry):
        r0 = pl.multiple_of(n * BLK, BLK)
        variant = jnp.where(n == 0, 0, jnp.where(n == nb - 1, 2, 1))
        for j in range(KV_A):
            cols = [slice(LANES * (2 * j + t), LANES * (2 * j + t + 1)) for t in range(2)]
            qs = jnp.concatenate(
                [ref[pl.ds(r0, BLK), c] for c in cols for ref in (qlo_ref, qhi_ref)], axis=0)
            s = _dot_nt(qs, kk_ref[j, pl.ds(r0, 3 * BLK), :]) + bias_ref[variant, j]
            mx = jnp.maximum(jnp.max(s, axis=-1, keepdims=True), sinks[j])
            p = jnp.exp2(s - mx).astype(BF16)
            res = _dot(p, vv_ref[j, pl.ds(r0, 3 * BLK), :])
            denom = res[:, LANES:LANES + 1] + jnp.exp2(sinks[j] - mx)
            o = res[:, :LANES] * (1.0 / denom)
            for t in range(2):
                o_ref[pl.ds(r0, BLK), cols[t]] = jnp.where(
                    lo, o[2 * t * BLK:(2 * t + 1) * BLK], o[(2 * t + 1) * BLK:(2 * t + 2) * BLK]
                ).astype(o_ref.dtype)
        return carry

    lax.fori_loop(0, nb, body, 0)


def _mixa(z3, sink, qg2, kg2, tab_a, layer):
    bsz = z3.shape[0]
    return pl.pallas_call(
        functools.partial(_mixa_kernel, layer=layer),
        grid=(bsz,),
        in_specs=[
            pl.BlockSpec(memory_space=pltpu.SMEM),
            pl.BlockSpec((None, SEQ, WA), lambda b: (b, 0, 0)),
            pl.BlockSpec((None, SEQ, 2 * LANES), lambda b: (b, 0, WA // (2 * LANES))),
            pl.BlockSpec((None, 1, LANES), lambda b: (layer, 0, 0)),
            pl.BlockSpec((None, 1, LANES), lambda b: (layer, 0, 0)),
            pl.BlockSpec((3, KV_A, HA // KV_A * BLK, 3 * BLK), lambda b: (0, 0, 0, 0)),
        ],
        out_specs=pl.BlockSpec((None, SEQ, WA), lambda b: (b, 0, 0)),
        out_shape=jax.ShapeDtypeStruct((bsz, SEQ, WA), BF16),
        scratch_shapes=[
            pltpu.VMEM((SEQ, WA), BF16),
            pltpu.VMEM((SEQ, WA), BF16),
            pltpu.VMEM((KV_A, SEQ + 2 * BLK, LANES), BF16),
            pltpu.VMEM((KV_A, SEQ + 2 * BLK, 2 * LANES), BF16),
        ],
        compiler_params=pltpu.CompilerParams(
            dimension_semantics=("parallel",), vmem_limit_bytes=VMEM_LIMIT),
        name=f"mixa_l{layer}",
    )(sink, z3, z3, qg2, kg2, tab_a)


def _mixb_kernel(far_ref, q_ref, k_ref, v_ref, qg_ref, kg_ref, lam_ref, sg_ref, tab_ref, o_ref,
                 q0_ref, q1_ref, ka_ref, va_ref, sa_ref, sb_ref, *, lam_init):
    head = pl.program_id(1)
    lo = _lo_mask()
    lane = lax.broadcasted_iota(jnp.int32, (1, LANES), 1)
    ka_ref[:, :LANES] = _half_rms(k_ref[...], kg_ref[...], lo).astype(BF16)
    key_blk = lax.broadcasted_iota(jnp.int32, (SEQ, LANES), 0) // BLK
    key_lane = lax.broadcasted_iota(jnp.int32, (SEQ, LANES), 1)
    onehot = (key_lane < 2 * NKB_B) & ((key_lane % NKB_B) == key_blk)
    ka_ref[:, LANES:] = jnp.where(onehot, 1.0, 0.0).astype(BF16)
    va_ref[:, :LANES] = v_ref[...].astype(BF16)
    va_ref[:, LANES:] = jnp.ones((SEQ, LANES), BF16)
    qn = _half_rms(q_ref[...], qg_ref[...], lo) * (SCALE * LOG2E)
    q0_ref[...] = jnp.where(lo, qn, 0.0).astype(BF16)
    q1_ref[...] = jnp.where(lo, 0.0, qn).astype(BF16)
    lv = lam_ref[...]
    lam = (jnp.exp(jnp.sum(lv[0:1] * lv[1:2], axis=-1, keepdims=True))
           - jnp.exp(jnp.sum(lv[2:3] * lv[3:4], axis=-1, keepdims=True)) + lam_init)
    far_left, far_right = far_ref[head, 0], far_ref[head, 1]
    n_tiles = SEQ // TQ_B
    blocks_per_tile = TQ_B // BLK
    blk_of_lane = lane % NKB_B

    def scores(i, s_ref):
        band_blk = min(max(blocks_per_tile * i - 1, 0), NKB_B - NBAND_B)
        variant = 0 if i == 0 else (2 if i == n_tiles - 1 else 1)
        far = jnp.where(blk_of_lane < band_blk, far_left,
                        jnp.where(blk_of_lane >= band_blk + NBAND_B, far_right, 0.0))
        far_hi = far.astype(BF16).astype(F32)
        far_vec = jnp.where(lane < NKB_B, far_hi, jnp.where(lane < 2 * NKB_B, far - far_hi, 0.0))
        far_q = jnp.broadcast_to(far_vec, (TQ_B, LANES)).astype(BF16)
        band = slice(band_blk * BLK, (band_blk + NBAND_B) * BLK)
        for c, qc_ref in enumerate((q0_ref, q1_ref)):
            qa = jnp.concatenate([qc_ref[i * TQ_B:(i + 1) * TQ_B, :], far_q], axis=1)
            s_ref[c] = _dot_nt(qa, ka_ref[...])
            s_ref[c, :, band] = s_ref[c, :, band] + tab_ref[variant]

    def attend(i, s_ref):
        res = []
        for c in range(2):
            s = s_ref[c]
            p = jnp.exp2(s - jnp.max(s, axis=-1, keepdims=True)).astype(BF16)
            res.append(_dot(p, va_ref[...]))
        c0 = 1.0 / res[0][:, LANES:LANES + 1]
        c1 = lam / res[1][:, LANES:LANES + 1]
        o = res[0][:, :LANES] * c0 - res[1][:, :LANES] * c1
        o = _rms(o, sg_ref[...]) * (1.0 - lam_init)
        o_ref[i * TQ_B:(i + 1) * TQ_B, :] = o.astype(o_ref.dtype)

    bufs = (sa_ref, sb_ref)
    scores(0, bufs[0])
    for i in range(n_tiles):
        if i + 1 < n_tiles:
            scores(i + 1, bufs[(i + 1) % 2])
        attend(i, bufs[i % 2])


def _mixb(z3, far_b, qg2, kg2, lam_vecs, subln_g, tab_b, layer):
    bsz = z3.shape[0]
    lam_init = 0.8 - 0.6 * math.exp(-0.3 * layer)
    qb0 = (WA + 2 * KV_A * HEAD_DIM) // LANES
    kb0 = qb0 + WB // LANES
    vb0 = kb0 + WB // LANES
    return pl.pallas_call(
        functools.partial(_mixb_kernel, lam_init=lam_init),
        grid=(bsz, HB),
        in_specs=[
            pl.BlockSpec(memory_space=pltpu.SMEM),
            pl.BlockSpec((None, SEQ, LANES), lambda b, h: (b, 0, qb0 + h)),
            pl.BlockSpec((None, SEQ, LANES), lambda b, h: (b, 0, kb0 + h)),
            pl.BlockSpec((None, SEQ, LANES), lambda b, h: (b, 0, vb0 + h)),
            pl.BlockSpec((None, 1, LANES), lambda b, h: (layer, 0, 0)),
            pl.BlockSpec((None, 1, LANES), lambda b, h: (layer, 0, 0)),
            pl.BlockSpec((None, 4, HEAD_DIM), lambda b, h: (layer, 0, 0)),
            pl.BlockSpec((None, 1, LANES), lambda b, h: (layer, 0, 0)),
            pl.BlockSpec((None, 3, TQ_B, NBAND_B * BLK), lambda b, h: (h, 0, 0, 0)),
        ],
        out_specs=pl.BlockSpec((None, SEQ, LANES), lambda b, h: (b, 0, h)),
        out_shape=jax.ShapeDtypeStruct((bsz, SEQ, WB), BF16),
        scratch_shapes=[
            pltpu.VMEM((SEQ, LANES), BF16),
            pltpu.VMEM((SEQ, LANES), BF16),
            pltpu.VMEM((SEQ, 2 * LANES), BF16),
            pltpu.VMEM((SEQ, 2 * LANES), BF16),
            pltpu.VMEM((2, TQ_B, SEQ), F32),
            pltpu.VMEM((2, TQ_B, SEQ), F32),
        ],
        compiler_params=pltpu.CompilerParams(
            dimension_semantics=("parallel", "arbitrary"), vmem_limit_bytes=VMEM_LIMIT),
        name=f"mixb_l{layer}",
    )(far_b, z3, z3, z3, qg2, kg2, lam_vecs, subln_g, tab_b)


def _merge_kernel(x_ref, oa_ref, ob_ref, g_ref, wg_ref, bg_ref, wa_ref, wb_ref, wo_ref, o_ref):
    x = x_ref[...]
    h = _rms(x, g_ref[...]).astype(BF16)
    gates = 1.0 / (1.0 + jnp.exp(-(_dot(h, wg_ref[...]) + bg_ref[...])))
    mix = (gates[:, :D_MODEL] * _dot(oa_ref[...], wa_ref[...])
           + gates[:, D_MODEL:] * _dot(ob_ref[...], wb_ref[...]))
    o_ref[...] = x + _dot(mix.astype(BF16), wo_ref[...])


def _merge(xf, oa, ob, ln_g, w_gate, b_gate, w_a, w_b, w_o, layer):
    n_tok = xf.shape[0]
    lsel = lambda i: (layer, 0, 0)
    return pl.pallas_call(
        _merge_kernel,
        grid=(n_tok // TM_PROJ,),
        in_specs=[
            pl.BlockSpec((TM_PROJ, D_MODEL), lambda i: (i, 0)),
            pl.BlockSpec((TM_PROJ, WA), lambda i: (i, 0)),
            pl.BlockSpec((TM_PROJ, WB), lambda i: (i, 0)),
            pl.BlockSpec((None, 1, D_MODEL), lsel),
            pl.BlockSpec((None, D_MODEL, 2 * D_MODEL), lsel),
            pl.BlockSpec((None, 1, 2 * D_MODEL), lsel),
            pl.BlockSpec((None, WA, D_MODEL), lsel),
            pl.BlockSpec((None, WB, D_MODEL), lsel),
            pl.BlockSpec((None, D_MODEL, D_MODEL), lsel),
        ],
        out_specs=pl.BlockSpec((TM_PROJ, D_MODEL), lambda i: (i, 0)),
        out_shape=jax.ShapeDtypeStruct((n_tok, D_MODEL), F32),
        compiler_params=pltpu.CompilerParams(
            dimension_semantics=("parallel",), vmem_limit_bytes=VMEM_LIMIT),
        name=f"merge_l{layer}",
    )(xf, oa, ob, ln_g, w_gate, b_gate, w_a, w_b, w_o)


def _ffn_kernel(xp_ref, xc_ref, xn_ref, g_ref, wup_ref, cw_ref, cb_ref, wdn_ref, o_ref,
                us_ref, act_ref):
    tiles_per_seq = SEQ // TM_FFN
    t = pl.program_id(0) % tiles_per_seq
    g = g_ref[...]
    xc = xc_ref[...]
    hp = jnp.where(t == 0, 0.0, _rms(xp_ref[...], g))
    hn = jnp.where(t == tiles_per_seq - 1, 0.0, _rms(xn_ref[...], g))
    h2 = jnp.concatenate([hp, _rms(xc, g), hn], axis=0).astype(BF16)

    def conv(part, col):
        w = cw_ref[:, col:col + TC_FFN]
        acc = cb_ref[:, col:col + TC_FFN]
        for j in range(CONV_WIDTH):
            acc = acc + us_ref[part, SUBLANES - 1 + j:SUBLANES - 1 + j + TM_FFN, :] * w[j:j + 1]
        return acc

    for c in range(D_FF // TC_FFN):
        cv, cg = c * TC_FFN, D_FF + c * TC_FFN
        us_ref[0] = _dot(h2, wup_ref[:, cv:cv + TC_FFN])
        us_ref[1] = _dot(h2, wup_ref[:, cg:cg + TC_FFN])
        val = conv(0, cv)
        gate = conv(1, cg)
        act = gate * (1.0 / (1.0 + jnp.exp(-gate))) * val
        act_ref[:, cv:cv + TC_FFN] = act.astype(BF16)
    o_ref[...] = xc + _dot(act_ref[...], wdn_ref[...])


def _ffn(xf, ln_g, w_up, conv_w, conv_b, w_down, layer):
    n_tok = xf.shape[0]
    lsel = lambda i: (layer, 0, 0)
    halo_per_tile = TM_FFN // SUBLANES
    n_halo = n_tok // SUBLANES
    return pl.pallas_call(
        _ffn_kernel,
        grid=(n_tok // TM_FFN,),
        in_specs=[
            pl.BlockSpec((SUBLANES, D_MODEL), lambda i: (jnp.maximum(i * halo_per_tile - 1, 0), 0)),
            pl.BlockSpec((TM_FFN, D_MODEL), lambda i: (i, 0)),
            pl.BlockSpec((SUBLANES, D_MODEL),
                         lambda i: (jnp.minimum((i + 1) * halo_per_tile, n_halo - 1), 0)),
            pl.BlockSpec((None, 1, D_MODEL), lsel),
            pl.BlockSpec((None, D_MODEL, 2 * D_FF), lsel),
            pl.BlockSpec((None, CONV_WIDTH, 2 * D_FF), lsel),
            pl.BlockSpec((None, 1, 2 * D_FF), lsel),
            pl.BlockSpec((None, D_FF, D_MODEL), lsel),
        ],
        out_specs=pl.BlockSpec((TM_FFN, D_MODEL), lambda i: (i, 0)),
        out_shape=jax.ShapeDtypeStruct((n_tok, D_MODEL), F32),
        scratch_shapes=[
            pltpu.VMEM((2, TM_FFN + 2 * SUBLANES, TC_FFN), F32),
            pltpu.VMEM((TM_FFN, D_FF), BF16),
        ],
        compiler_params=pltpu.CompilerParams(
            dimension_semantics=("parallel",), vmem_limit_bytes=VMEM_LIMIT),
        name=f"ffn_l{layer}",
    )(xf, xf, xf, ln_g, w_up, conv_w, conv_b, w_down)


def _t5_bucket(rel):
    half = N_BUCKETS // 2
    max_exact = half // 2
    ret = jnp.where(rel > 0, half, 0)
    n = jnp.abs(rel)
    nf = jnp.maximum(n, 1).astype(jnp.float32)
    large = max_exact + (jnp.log(nf / max_exact) / math.log(MAX_DIST / max_exact)
                         * (half - max_exact)).astype(jnp.int32)
    large = jnp.minimum(large, half - 1)
    return ret + jnp.where(n < max_exact, n, large)


def _expand_bias(bucket, tab):
    out = jnp.zeros((tab.shape[1],) + bucket.shape, F32)
    for b in range(N_BUCKETS):
        out = out + jnp.where(bucket[None] == b, tab[b][:, None, None], 0.0)
    return out


def _bias_tables(rel_bias):
    bias_a, bias_b = rel_bias[:, :HA], rel_bias[:, HA:]
    rel_np = np.arange(3 * BLK)[None, :] - BLK - np.arange(BLK)[:, None]
    base = _expand_bias(_t5_bucket(jnp.asarray(rel_np, jnp.int32)), bias_a)
    col = np.arange(3 * BLK)[None, :]
    in_win = np.abs(rel_np) <= WINDOW
    variants = []
    for valid_cols in (col >= BLK, col >= 0, col < 2 * BLK):
        variants.append(jnp.where(jnp.asarray(in_win & valid_cols)[None], base * LOG2E, NEG))
    tab_a = jnp.stack(variants).reshape(3, KV_A, HA // KV_A * BLK, 3 * BLK)
    rel_b = np.stack([np.arange(NBAND_B * BLK)[None, :] - d * BLK - np.arange(TQ_B)[:, None]
                      for d in range(3)])
    tab_b = _expand_bias(_t5_bucket(jnp.asarray(rel_b.reshape(3 * TQ_B, -1), jnp.int32)), bias_b)
    tab_b = tab_b.reshape(HB, 3, TQ_B, NBAND_B * BLK) * LOG2E
    far_b = jnp.stack([bias_b[N_BUCKETS // 2 - 1], bias_b[N_BUCKETS - 1]], axis=1) * LOG2E
    return tab_a, tab_b, far_b


def kernel(x, ln1_g, w_in, qn_a, kn_a, sink, qn_b, kn_b, lam_q1, lam_k1, lam_q2, lam_k2,
           subln_g, rel_bias, w_gate, b_gate, w_a_proj, w_b_proj, w_o, ln2_g, w_up,
           conv_w, conv_b, w_down):
    bsz, seq, d = x.shape
    assert (bsz, seq, d) == (BATCH, SEQ, D_MODEL)
    tab_a, tab_b, far_b = _bias_tables(rel_bias.astype(F32))
    row = lambda a: a.astype(F32)[:, None, :]
    twice = lambda a: jnp.concatenate([a, a], axis=-1).astype(F32)[:, None, :]
    ln1, ln2, sg = row(ln1_g), row(ln2_g), row(subln_g)
    qga, kga, qgb, kgb = twice(qn_a), twice(kn_a), twice(qn_b), twice(kn_b)
    lam_vecs = jnp.stack([lam_q1, lam_k1, lam_q2, lam_k2], axis=1).astype(F32)
    bg, cb = row(b_gate), row(conv_b)
    cw = conv_w.astype(F32)
    snk = sink.astype(F32)
    w_in_b, w_gate_b, w_a_b, w_b_b, w_o_b, w_up_b, w_down_b = (
        w.astype(BF16) for w in (w_in, w_gate, w_a_proj, w_b_proj, w_o, w_up, w_down))

    xf = x.astype(F32).reshape(bsz * seq, d)
    for layer in range(DEPTH):
        z3 = _inproj(xf, ln1, w_in_b, layer).reshape(bsz, seq, IN_W)
        oa = _mixa(z3, snk, qga, kga, tab_a, layer).reshape(bsz * seq, WA)
        ob = _mixb(z3, far_b, qgb, kgb, lam_vecs, sg, tab_b, layer).reshape(bsz * seq, WB)
        xf = _merge(xf, oa, ob, ln1, w_gate_b, bg, w_a_b, w_b_b, w_o_b, layer)
        xf = _ffn(xf, ln2, w_up_b, cw, cb, w_down_b, layer)
    return xf.reshape(bsz, seq, d).astype(x.dtype)
```

```python
import functools
import math

import numpy as np
import jax
import jax.numpy as jnp
from jax import lax
from jax.experimental import pallas as pl
from jax.experimental.pallas import tpu as pltpu

D_MODEL = 1024
BATCH = 8
SEQ = 2048
DEPTH = 4
HEAD_DIM = 64
HA = 8
KV_A = 2
HB = 4
WA = HA * HEAD_DIM
WB = HB * 2 * HEAD_DIM
WINDOW = 128
BLK = 128
N_BUCKETS = 32
MAX_DIST = 128
D_FF = 2816
CONV_WIDTH = 3
EPS = 1e-6
IN_W = WA + 2 * KV_A * HEAD_DIM + 3 * WB
SCALE = HEAD_DIM ** -0.5
NEG = -1e30

LANES = 128
SUBLANES = 8
VMEM_LIMIT = 48 * 1024 * 1024

TM_PROJ = 512
TM_FFN = 512
TC_FFN = 256
TQ_B = 512
NKB_B = SEQ // BLK
NBAND_B = TQ_B // BLK + 2
LOG2E = math.log2(math.e)
assert 2 * NKB_B <= LANES and SEQ // TQ_B >= 3 and MAX_DIST <= BLK

F32 = jnp.float32
BF16 = jnp.bfloat16


def _dot(a, b):
    return jnp.dot(a, b, preferred_element_type=F32)


def _dot_nt(a, b):
    return lax.dot_general(a, b, (((1,), (1,)), ((), ())), preferred_element_type=F32)


def _rms(x, g):
    return x * lax.rsqrt(jnp.mean(x * x, axis=-1, keepdims=True) + EPS) * g


def _half_rms(x, g2, lo):
    x2 = x * x
    s_all = jnp.sum(x2, axis=-1, keepdims=True)
    s_lo = jnp.sum(jnp.where(lo, x2, 0.0), axis=-1, keepdims=True)
    ss = jnp.where(lo, s_lo, s_all - s_lo)
    return x * lax.rsqrt(ss * (1.0 / HEAD_DIM) + EPS) * g2


def _group_ones(width):
    r = lax.broadcasted_iota(jnp.int32, (width, width), 0) // HEAD_DIM
    c = lax.broadcasted_iota(jnp.int32, (width, width), 1) // HEAD_DIM
    return jnp.where(r == c, 1.0, 0.0).astype(BF16)


def _head_rms(x, g, ones_bd):
    x2 = x * x
    hi = x2.astype(BF16)
    lo = (x2 - hi.astype(F32)).astype(BF16)
    ss = _dot(hi, ones_bd) + _dot(lo, ones_bd)
    return x * lax.rsqrt(ss * (1.0 / HEAD_DIM) + EPS) * g


def _lo_mask():
    return lax.broadcasted_iota(jnp.int32, (1, LANES), 1) < HEAD_DIM


def _inproj_kernel(x_ref, g_ref, w_ref, z_ref):
    h = _rms(x_ref[...], g_ref[...]).astype(BF16)
    z_ref[...] = _dot(h, w_ref[...])


def _inproj(xf, ln_g, w_in, layer):
    n_tok = xf.shape[0]
    return pl.pallas_call(
        _inproj_kernel,
        grid=(n_tok // TM_PROJ,),
        in_specs=[
            pl.BlockSpec((TM_PROJ, D_MODEL), lambda i: (i, 0)),
            pl.BlockSpec((None, 1, D_MODEL), lambda i: (layer, 0, 0)),
            pl.BlockSpec((None, D_MODEL, IN_W), lambda i: (layer, 0, 0)),
        ],
        out_specs=pl.BlockSpec((TM_PROJ, IN_W), lambda i: (i, 0)),
        out_shape=jax.ShapeDtypeStruct((n_tok, IN_W), F32),
        compiler_params=pltpu.CompilerParams(
            dimension_semantics=("parallel",), vmem_limit_bytes=VMEM_LIMIT),
        name=f"inproj_l{layer}",
    )(xf, ln_g, w_in)


def _mixa_kernel(sink_ref, q_ref, kv_ref, qg_ref, kg_ref, bias_ref, o_ref,
                 qlo_ref, qhi_ref, kk_ref, vv_ref, sa_ref, sb_ref, *, layer):
    nb = SEQ // BLK
    group = HA // KV_A
    lo = _lo_mask()

    kv = kv_ref[...]
    k = _head_rms(kv[:, :LANES], kg_ref[...], _group_ones(LANES))
    v = kv[:, LANES:]
    kr = pltpu.roll(k, HEAD_DIM, 1)
    vr = pltpu.roll(v, HEAD_DIM, 1)
    for j in range(KV_A):
        for edge in (slice(0, BLK), slice(SEQ + BLK, SEQ + 2 * BLK)):
            kk_ref[j, edge, :] = jnp.zeros((BLK, LANES), BF16)
            vv_ref[j, edge, :] = jnp.zeros((BLK, 2 * LANES), BF16)
        vv_ref[j, BLK:BLK + SEQ, LANES:] = jnp.ones((SEQ, LANES), BF16)
    kk_ref[0, BLK:BLK + SEQ, :] = jnp.where(lo, k, kr).astype(BF16)
    kk_ref[1, BLK:BLK + SEQ, :] = jnp.where(lo, kr, k).astype(BF16)
    vv_ref[0, BLK:BLK + SEQ, :LANES] = jnp.where(lo, v, vr).astype(BF16)
    vv_ref[1, BLK:BLK + SEQ, :LANES] = jnp.where(lo, vr, v).astype(BF16)
    ones_pair = _group_ones(2 * LANES)
    lo_pair = jnp.concatenate([lo, lo], axis=1)
    qg_pair = jnp.concatenate([qg_ref[...], qg_ref[...]], axis=1)
    for j in range(KV_A):
        pair = slice(2 * LANES * j, 2 * LANES * (j + 1))
        qn = _head_rms(q_ref[:, pair], qg_pair, ones_pair) * (SCALE * LOG2E)
        qlo_ref[:, pair] = jnp.where(lo_pair, qn, 0.0).astype(BF16)
        qhi_ref[:, pair] = jnp.where(lo_pair, 0.0, qn).astype(BF16)

    head_row = lax.broadcasted_iota(jnp.int32, (group * BLK, 1), 0) // BLK
    sinks = []
    for j in range(KV_A):
        snk = jnp.zeros((group * BLK, 1), F32)
        for g in range(group):
            snk = jnp.where(head_row == g, sink_ref[layer, group * j + g] * LOG2E, snk)
        sinks.append(snk)

    def cols(j):
        return [slice(LANES * (2 * j + t), LANES * (2 * j + t + 1)) for t in range(2)]

    def scores(n, s_ref):
        variant = 0 if n == 0 else (2 if n == nb - 1 else 1)
        rows = slice(n * BLK, (n + 1) * BLK)
        window = slice(n * BLK, (n + 3) * BLK)
        for j in range(KV_A):
            qs = jnp.concatenate(
                [ref[rows, c] for c in cols(j) for ref in (qlo_ref, qhi_ref)], axis=0)
            s_ref[j] = _dot_nt(qs, kk_ref[j, window, :]) + bias_ref[variant, j]

    def attend(n, s_ref):
        rows = slice(n * BLK, (n + 1) * BLK)
        window = slice(n * BLK, (n + 3) * BLK)
        for j in range(KV_A):
            s = s_ref[j]
            mx = jnp.maximum(jnp.max(s, axis=-1, keepdims=True), sinks[j])
            p = jnp.exp2(s - mx).astype(BF16)
            res = _dot(p, vv_ref[j, window, :])
            denom = res[:, LANES:LANES + 1] + jnp.exp2(sinks[j] - mx)
            o = res[:, :LANES] * (1.0 / denom)
            for t, c in enumerate(cols(j)):
                o_ref[rows, c] = jnp.where(
                    lo, o[2 * t * BLK:(2 * t + 1) * BLK], o[(2 * t + 1) * BLK:(2 * t + 2) * BLK]
                ).astype(o_ref.dtype)

    bufs = (sa_ref, sb_ref)
    scores(0, bufs[0])
    for n in range(nb):
        if n + 1 < nb:
            scores(n + 1, bufs[(n + 1) % 2])
        attend(n, bufs[n % 2])


def _mixa(z3, sink, qg2, kg2, tab_a, layer):
    bsz = z3.shape[0]
    return pl.pallas_call(
        functools.partial(_mixa_kernel, layer=layer),
        grid=(bsz,),
        in_specs=[
            pl.BlockSpec(memory_space=pltpu.SMEM),
            pl.BlockSpec((None, SEQ, WA), lambda b: (b, 0, 0)),
            pl.BlockSpec((None, SEQ, 2 * LANES), lambda b: (b, 0, WA // (2 * LANES))),
            pl.BlockSpec((None, 1, LANES), lambda b: (layer, 0, 0)),
            pl.BlockSpec((None, 1, LANES), lambda b: (layer, 0, 0)),
            pl.BlockSpec((3, KV_A, HA // KV_A * BLK, 3 * BLK), lambda b: (0, 0, 0, 0)),
        ],
        out_specs=pl.BlockSpec((None, SEQ, WA), lambda b: (b, 0, 0)),
        out_shape=jax.ShapeDtypeStruct((bsz, SEQ, WA), BF16),
        scratch_shapes=[
            pltpu.VMEM((SEQ, WA), BF16),
            pltpu.VMEM((SEQ, WA), BF16),
            pltpu.VMEM((KV_A, SEQ + 2 * BLK, LANES), BF16),
            pltpu.VMEM((KV_A, SEQ + 2 * BLK, 2 * LANES), BF16),
            pltpu.VMEM((KV_A, HA // KV_A * BLK, 3 * BLK), F32),
            pltpu.VMEM((KV_A, HA // KV_A * BLK, 3 * BLK), F32),
        ],
        compiler_params=pltpu.CompilerParams(
            dimension_semantics=("parallel",), vmem_limit_bytes=VMEM_LIMIT),
        name=f"mixa_l{layer}",
    )(sink, z3, z3, qg2, kg2, tab_a)


def _mixb_kernel(far_ref, q_ref, k_ref, v_ref, qg_ref, kg_ref, lam_ref, sg_ref, tab_ref, o_ref,
                 q0_ref, q1_ref, ka_ref, va_ref, sa_ref, sb_ref, *, lam_init):
    head = pl.program_id(1)
    lo = _lo_mask()
    lane = lax.broadcasted_iota(jnp.int32, (1, LANES), 1)
    ka_ref[:, :LANES] = _half_rms(k_ref[...], kg_ref[...], lo).astype(BF16)
    key_blk = lax.broadcasted_iota(jnp.int32, (SEQ, LANES), 0) // BLK
    key_lane = lax.broadcasted_iota(jnp.int32, (SEQ, LANES), 1)
    onehot = (key_lane < 2 * NKB_B) & ((key_lane % NKB_B) == key_blk)
    ka_ref[:, LANES:] = jnp.where(onehot, 1.0, 0.0).astype(BF16)
    va_ref[:, :LANES] = v_ref[...].astype(BF16)
    va_ref[:, LANES:] = jnp.ones((SEQ, LANES), BF16)
    qn = _half_rms(q_ref[...], qg_ref[...], lo) * (SCALE * LOG2E)
    q0_ref[...] = jnp.where(lo, qn, 0.0).astype(BF16)
    q1_ref[...] = jnp.where(lo, 0.0, qn).astype(BF16)
    lv = lam_ref[...]
    lam = (jnp.exp(jnp.sum(lv[0:1] * lv[1:2], axis=-1, keepdims=True))
           - jnp.exp(jnp.sum(lv[2:3] * lv[3:4], axis=-1, keepdims=True)) + lam_init)
    far_left, far_right = far_ref[head, 0], far_ref[head, 1]
    n_tiles = SEQ // TQ_B
    blocks_per_tile = TQ_B // BLK
    blk_of_lane = lane % NKB_B

    def scores(i, s_ref):
        band_blk = min(max(blocks_per_tile * i - 1, 0), NKB_B - NBAND_B)
        lead = blocks_per_tile * i - band_blk
        tab = slice((2 - lead) * BLK, (2 - lead + NBAND_B) * BLK)
        far = jnp.where(blk_of_lane < band_blk, far_left,
                        jnp.where(blk_of_lane >= band_blk + NBAND_B, far_right, 0.0))
        far_hi = far.astype(BF16).astype(F32)
        far_vec = jnp.where(lane < NKB_B, far_hi, jnp.where(lane < 2 * NKB_B, far - far_hi, 0.0))
        far_q = jnp.broadcast_to(far_vec, (TQ_B, LANES)).astype(BF16)
        band = slice(band_blk * BLK, (band_blk + NBAND_B) * BLK)
        for c, qc_ref in enumerate((q0_ref, q1_ref)):
            qa = jnp.concatenate([qc_ref[i * TQ_B:(i + 1) * TQ_B, :], far_q], axis=1)
            s_ref[c] = _dot_nt(qa, ka_ref[...])
            s_ref[c, :, band] = s_ref[c, :, band] + tab_ref[:, tab]

    def attend(i, s_ref):
        res = []
        for c in range(2):
            s = s_ref[c]
            p = jnp.exp2(s - jnp.max(s, axis=-1, keepdims=True)).astype(BF16)
            res.append(_dot(p, va_ref[...]))
        c0 = 1.0 / res[0][:, LANES:LANES + 1]
        c1 = lam / res[1][:, LANES:LANES + 1]
        o = res[0][:, :LANES] * c0 - res[1][:, :LANES] * c1
        o = _rms(o, sg_ref[...]) * (1.0 - lam_init)
        o_ref[i * TQ_B:(i + 1) * TQ_B, :] = o.astype(o_ref.dtype)

    bufs = (sa_ref, sb_ref)
    scores(0, bufs[0])
    for i in range(n_tiles):
        if i + 1 < n_tiles:
            scores(i + 1, bufs[(i + 1) % 2])
        attend(i, bufs[i % 2])


def _mixb(z3, far_b, qg2, kg2, lam_vecs, subln_g, tab_b, layer):
    bsz = z3.shape[0]
    lam_init = 0.8 - 0.6 * math.exp(-0.3 * layer)
    qb0 = (WA + 2 * KV_A * HEAD_DIM) // LANES
    kb0 = qb0 + WB // LANES
    vb0 = kb0 + WB // LANES
    return pl.pallas_call(
        functools.partial(_mixb_kernel, lam_init=lam_init),
        grid=(bsz, HB),
        in_specs=[
            pl.BlockSpec(memory_space=pltpu.SMEM),
            pl.BlockSpec((None, SEQ, LANES), lambda b, h: (b, 0, qb0 + h)),
            pl.BlockSpec((None, SEQ, LANES), lambda b, h: (b, 0, kb0 + h)),
            pl.BlockSpec((None, SEQ, LANES), lambda b, h: (b, 0, vb0 + h)),
            pl.BlockSpec((None, 1, LANES), lambda b, h: (layer, 0, 0)),
            pl.BlockSpec((None, 1, LANES), lambda b, h: (layer, 0, 0)),
            pl.BlockSpec((None, 4, HEAD_DIM), lambda b, h: (layer, 0, 0)),
            pl.BlockSpec((None, 1, LANES), lambda b, h: (layer, 0, 0)),
            pl.BlockSpec((None, TQ_B, (NBAND_B + 2) * BLK), lambda b, h: (h, 0, 0)),
        ],
        out_specs=pl.BlockSpec((None, SEQ, LANES), lambda b, h: (b, 0, h)),
        out_shape=jax.ShapeDtypeStruct((bsz, SEQ, WB), BF16),
        scratch_shapes=[
            pltpu.VMEM((SEQ, LANES), BF16),
            pltpu.VMEM((SEQ, LANES), BF16),
            pltpu.VMEM((SEQ, 2 * LANES), BF16),
            pltpu.VMEM((SEQ, 2 * LANES), BF16),
            pltpu.VMEM((2, TQ_B, SEQ), F32),
            pltpu.VMEM((2, TQ_B, SEQ), F32),
        ],
        compiler_params=pltpu.CompilerParams(
            dimension_semantics=("parallel", "arbitrary"), vmem_limit_bytes=VMEM_LIMIT),
        name=f"mixb_l{layer}",
    )(far_b, z3, z3, z3, qg2, kg2, lam_vecs, subln_g, tab_b)


def _merge_kernel(x_ref, oa_ref, ob_ref, g_ref, wg_ref, bg_ref, wa_ref, wb_ref, wo_ref, o_ref):
    x = x_ref[...]
    h = _rms(x, g_ref[...]).astype(BF16)
    gates = 1.0 / (1.0 + jnp.exp(-(_dot(h, wg_ref[...]) + bg_ref[...])))
    mix = (gates[:, :D_MODEL] * _dot(oa_ref[...], wa_ref[...])
           + gates[:, D_MODEL:] * _dot(ob_ref[...], wb_ref[...]))
    o_ref[...] = x + _dot(mix.astype(BF16), wo_ref[...])


def _merge(xf, oa, ob, ln_g, w_gate, b_gate, w_a, w_b, w_o, layer):
    n_tok = xf.shape[0]
    lsel = lambda i: (layer, 0, 0)
    return pl.pallas_call(
        _merge_kernel,
        grid=(n_tok // TM_PROJ,),
        in_specs=[
            pl.BlockSpec((TM_PROJ, D_MODEL), lambda i: (i, 0)),
            pl.BlockSpec((TM_PROJ, WA), lambda i: (i, 0)),
            pl.BlockSpec((TM_PROJ, WB), lambda i: (i, 0)),
            pl.BlockSpec((None, 1, D_MODEL), lsel),
            pl.BlockSpec((None, D_MODEL, 2 * D_MODEL), lsel),
            pl.BlockSpec((None, 1, 2 * D_MODEL), lsel),
            pl.BlockSpec((None, WA, D_MODEL), lsel),
            pl.BlockSpec((None, WB, D_MODEL), lsel),
            pl.BlockSpec((None, D_MODEL, D_MODEL), lsel),
        ],
        out_specs=pl.BlockSpec((TM_PROJ, D_MODEL), lambda i: (i, 0)),
        out_shape=jax.ShapeDtypeStruct((n_tok, D_MODEL), F32),
        compiler_params=pltpu.CompilerParams(
            dimension_semantics=("parallel",), vmem_limit_bytes=VMEM_LIMIT),
        name=f"merge_l{layer}",
    )(xf, oa, ob, ln_g, w_gate, b_gate, w_a, w_b, w_o)


def _ffn_kernel(xp_ref, xc_ref, xn_ref, g_ref, wup_ref, cw_ref, cb_ref, wdn_ref, o_ref,
                act_ref):
    tiles_per_seq = SEQ // TM_FFN
    t = pl.program_id(0) % tiles_per_seq
    g = g_ref[...]
    xc = xc_ref[...]
    hp = jnp.where(t == 0, 0.0, _rms(xp_ref[...], g))
    hn = jnp.where(t == tiles_per_seq - 1, 0.0, _rms(xn_ref[...], g))
    h2 = jnp.concatenate([hp, _rms(xc, g), hn], axis=0).astype(BF16)

    rows = TM_FFN + 2 * SUBLANES
    centre = slice(SUBLANES, SUBLANES + TM_FFN)

    def conv(col):
        u = _dot(h2, wup_ref[:, col:col + TC_FFN])
        w = cw_ref[:, col:col + TC_FFN]
        prev = pltpu.roll(u, 1, 0)[centre]
        nxt = pltpu.roll(u, rows - 1, 0)[centre]
        return ((cb_ref[:, col:col + TC_FFN] + prev * w[0:1]) + u[centre] * w[1:2]) + nxt * w[2:3]

    for c in range(D_FF // TC_FFN):
        cv, cg = c * TC_FFN, D_FF + c * TC_FFN
        val = conv(cv)
        gate = conv(cg)
        act = gate * (1.0 / (1.0 + jnp.exp(-gate))) * val
        act_ref[:, cv:cv + TC_FFN] = act.astype(BF16)
    o_ref[...] = xc + _dot(act_ref[...], wdn_ref[...])


def _ffn(xf, ln_g, w_up, conv_w, conv_b, w_down, layer):
    n_tok = xf.shape[0]
    lsel = lambda i: (layer, 0, 0)
    halo_per_tile = TM_FFN // SUBLANES
    n_halo = n_tok // SUBLANES
    return pl.pallas_call(
        _ffn_kernel,
        grid=(n_tok // TM_FFN,),
        in_specs=[
            pl.BlockSpec((SUBLANES, D_MODEL), lambda i: (jnp.maximum(i * halo_per_tile - 1, 0), 0)),
            pl.BlockSpec((TM_FFN, D_MODEL), lambda i: (i, 0)),
            pl.BlockSpec((SUBLANES, D_MODEL),
                         lambda i: (jnp.minimum((i + 1) * halo_per_tile, n_halo - 1), 0)),
            pl.BlockSpec((None, 1, D_MODEL), lsel),
            pl.BlockSpec((None, D_MODEL, 2 * D_FF), lsel),
            pl.BlockSpec((None, CONV_WIDTH, 2 * D_FF), lsel),
            pl.BlockSpec((None, 1, 2 * D_FF), lsel),
            pl.BlockSpec((None, D_FF, D_MODEL), lsel),
        ],
        out_specs=pl.BlockSpec((TM_FFN, D_MODEL), lambda i: (i, 0)),
        out_shape=jax.ShapeDtypeStruct((n_tok, D_MODEL), F32),
        scratch_shapes=[
            pltpu.VMEM((TM_FFN, D_FF), BF16),
        ],
        compiler_params=pltpu.CompilerParams(
            dimension_semantics=("parallel",), vmem_limit_bytes=VMEM_LIMIT),
        name=f"ffn_l{layer}",
    )(xf, xf, xf, ln_g, w_up, conv_w, conv_b, w_down)


def _t5_bucket(rel):
    half = N_BUCKETS // 2
    max_exact = half // 2
    ret = jnp.where(rel > 0, half, 0)
    n = jnp.abs(rel)
    nf = jnp.maximum(n, 1).astype(jnp.float32)
    large = max_exact + (jnp.log(nf / max_exact) / math.log(MAX_DIST / max_exact)
                         * (half - max_exact)).astype(jnp.int32)
    large = jnp.minimum(large, half - 1)
    return ret + jnp.where(n < max_exact, n, large)


def _expand_bias(bucket, tab):
    out = jnp.zeros((tab.shape[1],) + bucket.shape, F32)
    for b in range(N_BUCKETS):
        out = out + jnp.where(bucket[None] == b, tab[b][:, None, None], 0.0)
    return out


def _bias_tables(rel_bias):
    bias_a, bias_b = rel_bias[:, :HA], rel_bias[:, HA:]
    rel_np = np.arange(3 * BLK)[None, :] - BLK - np.arange(BLK)[:, None]
    base = _expand_bias(_t5_bucket(jnp.asarray(rel_np, jnp.int32)), bias_a)
    col = np.arange(3 * BLK)[None, :]
    in_win = np.abs(rel_np) <= WINDOW
    variants = []
    for valid_cols in (col >= BLK, col >= 0, col < 2 * BLK):
        variants.append(jnp.where(jnp.asarray(in_win & valid_cols)[None], base * LOG2E, NEG))
    tab_a = jnp.stack(variants).reshape(3, KV_A, HA // KV_A * BLK, 3 * BLK)
    rel_b = np.arange((NBAND_B + 2) * BLK)[None, :] - 2 * BLK - np.arange(TQ_B)[:, None]
    tab_b = _expand_bias(_t5_bucket(jnp.asarray(rel_b, jnp.int32)), bias_b) * LOG2E
    far_b = jnp.stack([bias_b[N_BUCKETS // 2 - 1], bias_b[N_BUCKETS - 1]], axis=1) * LOG2E
    return tab_a, tab_b, far_b


def kernel(x, ln1_g, w_in, qn_a, kn_a, sink, qn_b, kn_b, lam_q1, lam_k1, lam_q2, lam_k2,
           subln_g, rel_bias, w_gate, b_gate, w_a_proj, w_b_proj, w_o, ln2_g, w_up,
           conv_w, conv_b, w_down):
    bsz, seq, d = x.shape
    assert (bsz, seq, d) == (BATCH, SEQ, D_MODEL)
    tab_a, tab_b, far_b = _bias_tables(rel_bias.astype(F32))
    row = lambda a: a.astype(F32)[:, None, :]
    twice = lambda a: jnp.concatenate([a, a], axis=-1).astype(F32)[:, None, :]
    ln1, ln2, sg = row(ln1_g), row(ln2_g), row(subln_g)
    qga, kga, qgb, kgb = twice(qn_a), twice(kn_a), twice(qn_b), twice(kn_b)
    lam_vecs = jnp.stack([lam_q1, lam_k1, lam_q2, lam_k2], axis=1).astype(F32)
    bg, cb = row(b_gate), row(conv_b)
    cw = conv_w.astype(F32)
    snk = sink.astype(F32)
    w_in_b, w_gate_b, w_a_b, w_b_b, w_o_b, w_up_b, w_down_b = (
        w.astype(BF16) for w in (w_in, w_gate, w_a_proj, w_b_proj, w_o, w_up, w_down))

    xf = x.astype(F32).reshape(bsz * seq, d)
    for layer in range(DEPTH):
        z3 = _inproj(xf, ln1, w_in_b, layer).reshape(bsz, seq, IN_W)
        oa = _mixa(z3, snk, qga, kga, tab_a, layer).reshape(bsz * seq, WA)
        ob = _mixb(z3, far_b, qgb, kgb, lam_vecs, sg, tab_b, layer).reshape(bsz * seq, WB)
        xf = _merge(xf, oa, ob, ln1, w_gate_b, bg, w_a_b, w_b_b, w_o_b, layer)
        xf = _ffn(xf, ln2, w_up_b, cw, cb, w_down_b, layer)
    return xf.reshape(bsz, seq, d).astype(x.dtype)
```

```python
import functools
import math

import numpy as np
import jax
import jax.numpy as jnp
from jax import lax
from jax.experimental import pallas as pl
from jax.experimental.pallas import tpu as pltpu

D_MODEL = 1024
BATCH = 8
SEQ = 2048
DEPTH = 4
HEAD_DIM = 64
HA = 8
KV_A = 2
HB = 4
WA = HA * HEAD_DIM
WB = HB * 2 * HEAD_DIM
WINDOW = 128
BLK = 128
N_BUCKETS = 32
MAX_DIST = 128
D_FF = 2816
CONV_WIDTH = 3
EPS = 1e-6
IN_W = WA + 2 * KV_A * HEAD_DIM + 3 * WB
QB0 = WA + 2 * KV_A * HEAD_DIM
VB0 = QB0 + 2 * WB
SCALE = HEAD_DIM ** -0.5
NEG = -1e30

LANES = 128
SUBLANES = 8
VMEM_LIMIT = 48 * 1024 * 1024

TM_PROJ = 512
TM_FFN = 512
TC_FFN = 256
TQ_B = 512
NKB_B = SEQ // BLK
NBAND_B = TQ_B // BLK + 2
LOG2E = math.log2(math.e)
assert 2 * NKB_B <= LANES and SEQ // TQ_B >= 3 and MAX_DIST <= BLK

F32 = jnp.float32
BF16 = jnp.bfloat16


def _dot(a, b):
    return jnp.dot(a, b, preferred_element_type=F32)


def _dot_nt(a, b):
    return lax.dot_general(a, b, (((1,), (1,)), ((), ())), preferred_element_type=F32)


def _inv_rms(x):
    return lax.rsqrt(jnp.mean(x * x, axis=-1, keepdims=True) + EPS)


def _rms(x, g):
    return x * _inv_rms(x) * g


def _half_rms(x, g2, lo):
    x2 = x * x
    s_all = jnp.sum(x2, axis=-1, keepdims=True)
    s_lo = jnp.sum(jnp.where(lo, x2, 0.0), axis=-1, keepdims=True)
    ss = jnp.where(lo, s_lo, s_all - s_lo)
    return x * lax.rsqrt(ss * (1.0 / HEAD_DIM) + EPS) * g2


def _group_ones(width):
    r = lax.broadcasted_iota(jnp.int32, (width, width), 0) // HEAD_DIM
    c = lax.broadcasted_iota(jnp.int32, (width, width), 1) // HEAD_DIM
    return jnp.where(r == c, 1.0, 0.0).astype(BF16)


def _head_rms(x, g, ones_bd):
    x2 = x * x
    hi = x2.astype(BF16)
    lo = (x2 - hi.astype(F32)).astype(BF16)
    ss = _dot(hi, ones_bd) + _dot(lo, ones_bd)
    return x * lax.rsqrt(ss * (1.0 / HEAD_DIM) + EPS) * g


def _lo_mask():
    return lax.broadcasted_iota(jnp.int32, (1, LANES), 1) < HEAD_DIM


def _inproj_kernel(x_ref, g_ref, w_ref, z_ref):
    x = x_ref[...]
    z_ref[...] = _dot((x * g_ref[...]).astype(BF16), w_ref[...]) * _inv_rms(x)


def _inproj(xf, ln_g, w_in, layer):
    n_tok = xf.shape[0]
    return pl.pallas_call(
        _inproj_kernel,
        grid=(n_tok // TM_PROJ,),
        in_specs=[
            pl.BlockSpec((TM_PROJ, D_MODEL), lambda i: (i, 0)),
            pl.BlockSpec((None, 1, D_MODEL), lambda i: (layer, 0, 0)),
            pl.BlockSpec((None, D_MODEL, IN_W), lambda i: (layer, 0, 0)),
        ],
        out_specs=pl.BlockSpec((TM_PROJ, IN_W), lambda i: (i, 0)),
        out_shape=jax.ShapeDtypeStruct((n_tok, IN_W), F32),
        compiler_params=pltpu.CompilerParams(
            dimension_semantics=("parallel",), vmem_limit_bytes=VMEM_LIMIT),
        name=f"inproj_l{layer}",
    )(xf, ln_g, w_in)


def _mixa_kernel(sink_ref, q_ref, kv_ref, qg_ref, kg_ref, bias_ref, o_ref,
                 qlo_ref, qhi_ref, kk_ref, vv_ref, sa_ref, sb_ref, *, layer):
    nb = SEQ // BLK
    group = HA // KV_A
    lo = _lo_mask()

    kv = kv_ref[...]
    k = _head_rms(kv[:, :LANES], kg_ref[...], _group_ones(LANES))
    v = kv[:, LANES:]
    kr = pltpu.roll(k, HEAD_DIM, 1)
    vr = pltpu.roll(v, HEAD_DIM, 1)
    for j in range(KV_A):
        for edge in (slice(0, BLK), slice(SEQ + BLK, SEQ + 2 * BLK)):
            kk_ref[j, edge, :] = jnp.zeros((BLK, LANES), BF16)
            vv_ref[j, edge, :] = jnp.zeros((BLK, 2 * LANES), BF16)
        vv_ref[j, BLK:BLK + SEQ, LANES:] = jnp.ones((SEQ, LANES), BF16)
    kk_ref[0, BLK:BLK + SEQ, :] = jnp.where(lo, k, kr).astype(BF16)
    kk_ref[1, BLK:BLK + SEQ, :] = jnp.where(lo, kr, k).astype(BF16)
    vv_ref[0, BLK:BLK + SEQ, :LANES] = jnp.where(lo, v, vr).astype(BF16)
    vv_ref[1, BLK:BLK + SEQ, :LANES] = jnp.where(lo, vr, v).astype(BF16)
    ones_pair = _group_ones(2 * LANES)
    lo_pair = jnp.concatenate([lo, lo], axis=1)
    qg_pair = jnp.concatenate([qg_ref[...], qg_ref[...]], axis=1)
    for j in range(KV_A):
        pair = slice(2 * LANES * j, 2 * LANES * (j + 1))
        qn = _head_rms(q_ref[:, pair], qg_pair, ones_pair) * (SCALE * LOG2E)
        qlo_ref[:, pair] = jnp.where(lo_pair, qn, 0.0).astype(BF16)
        qhi_ref[:, pair] = jnp.where(lo_pair, 0.0, qn).astype(BF16)

    head_row = lax.broadcasted_iota(jnp.int32, (group * BLK, 1), 0) // BLK
    sinks = []
    for j in range(KV_A):
        snk = jnp.zeros((group * BLK, 1), F32)
        for g in range(group):
            snk = jnp.where(head_row == g, sink_ref[layer, group * j + g] * LOG2E, snk)
        sinks.append(snk)

    def cols(j):
        return [slice(LANES * (2 * j + t), LANES * (2 * j + t + 1)) for t in range(2)]

    def scores(n, s_ref):
        variant = 0 if n == 0 else (2 if n == nb - 1 else 1)
        rows = slice(n * BLK, (n + 1) * BLK)
        window = slice(n * BLK, (n + 3) * BLK)
        for j in range(KV_A):
            qs = jnp.concatenate(
                [ref[rows, c] for c in cols(j) for ref in (qlo_ref, qhi_ref)], axis=0)
            s_ref[j] = _dot_nt(qs, kk_ref[j, window, :]) + bias_ref[variant, j]

    def attend(n, s_ref):
        rows = slice(n * BLK, (n + 1) * BLK)
        window = slice(n * BLK, (n + 3) * BLK)
        for j in range(KV_A):
            s = s_ref[j]
            mx = jnp.maximum(jnp.max(s, axis=-1, keepdims=True), sinks[j])
            p = jnp.exp2(s - mx).astype(BF16)
            res = _dot(p, vv_ref[j, window, :])
            denom = res[:, LANES:LANES + 1] + jnp.exp2(sinks[j] - mx)
            o = res[:, :LANES] * (1.0 / denom)
            for t, c in enumerate(cols(j)):
                o_ref[rows, c] = jnp.where(
                    lo, o[2 * t * BLK:(2 * t + 1) * BLK], o[(2 * t + 1) * BLK:(2 * t + 2) * BLK]
                ).astype(o_ref.dtype)

    bufs = (sa_ref, sb_ref)
    scores(0, bufs[0])
    for n in range(nb):
        if n + 1 < nb:
            scores(n + 1, bufs[(n + 1) % 2])
        attend(n, bufs[n % 2])


def _mixa(z3, sink, qg2, kg2, tab_a, layer):
    bsz = z3.shape[0]
    return pl.pallas_call(
        functools.partial(_mixa_kernel, layer=layer),
        grid=(bsz,),
        in_specs=[
            pl.BlockSpec(memory_space=pltpu.SMEM),
            pl.BlockSpec((None, SEQ, WA), lambda b: (b, 0, 0)),
            pl.BlockSpec((None, SEQ, 2 * LANES), lambda b: (b, 0, WA // (2 * LANES))),
            pl.BlockSpec((None, 1, LANES), lambda b: (layer, 0, 0)),
            pl.BlockSpec((None, 1, LANES), lambda b: (layer, 0, 0)),
            pl.BlockSpec((3, KV_A, HA // KV_A * BLK, 3 * BLK), lambda b: (0, 0, 0, 0)),
        ],
        out_specs=pl.BlockSpec((None, SEQ, WA), lambda b: (b, 0, 0)),
        out_shape=jax.ShapeDtypeStruct((bsz, SEQ, WA), BF16),
        scratch_shapes=[
            pltpu.VMEM((SEQ, WA), BF16),
            pltpu.VMEM((SEQ, WA), BF16),
            pltpu.VMEM((KV_A, SEQ + 2 * BLK, LANES), BF16),
            pltpu.VMEM((KV_A, SEQ + 2 * BLK, 2 * LANES), BF16),
            pltpu.VMEM((KV_A, HA // KV_A * BLK, 3 * BLK), F32),
            pltpu.VMEM((KV_A, HA // KV_A * BLK, 3 * BLK), F32),
        ],
        compiler_params=pltpu.CompilerParams(
            dimension_semantics=("parallel",), vmem_limit_bytes=VMEM_LIMIT),
        name=f"mixa_l{layer}",
    )(sink, z3, z3, qg2, kg2, tab_a)


def _mixb_kernel(far_ref, q_ref, k_ref, v_ref, qg_ref, kg_ref, lam_ref, sg_ref, tab_ref, o_ref,
                 q0_ref, q1_ref, ka_ref, va_ref, sa_ref, sb_ref, pa_ref, pb_ref, *, lam_init):
    head = pl.program_id(1)
    lo = _lo_mask()
    lane = lax.broadcasted_iota(jnp.int32, (1, LANES), 1)
    ka_ref[:, :LANES] = _half_rms(k_ref[...], kg_ref[...], lo).astype(BF16)
    key_blk = lax.broadcasted_iota(jnp.int32, (SEQ, LANES), 0) // BLK
    key_lane = lax.broadcasted_iota(jnp.int32, (SEQ, LANES), 1)
    onehot = (key_lane < 2 * NKB_B) & ((key_lane % NKB_B) == key_blk)
    ka_ref[:, LANES:] = jnp.where(onehot, 1.0, 0.0).astype(BF16)
    va_ref[:, :LANES] = v_ref[...].astype(BF16)
    va_ref[:, LANES:] = jnp.ones((SEQ, LANES), BF16)
    qn = _half_rms(q_ref[...], qg_ref[...], lo) * (SCALE * LOG2E)
    q0_ref[...] = jnp.where(lo, qn, 0.0).astype(BF16)
    q1_ref[...] = jnp.where(lo, 0.0, qn).astype(BF16)
    lv = lam_ref[...]
    lam = (jnp.exp(jnp.sum(lv[0:1] * lv[1:2], axis=-1, keepdims=True))
           - jnp.exp(jnp.sum(lv[2:3] * lv[3:4], axis=-1, keepdims=True)) + lam_init)
    far_left, far_right = far_ref[head, 0], far_ref[head, 1]
    n_tiles = SEQ // TQ_B
    blocks_per_tile = TQ_B // BLK
    blk_of_lane = lane % NKB_B

    def scores(i, s_ref):
        band_blk = min(max(blocks_per_tile * i - 1, 0), NKB_B - NBAND_B)
        lead = blocks_per_tile * i - band_blk
        tab = slice((2 - lead) * BLK, (2 - lead + NBAND_B) * BLK)
        far = jnp.where(blk_of_lane < band_blk, far_left,
                        jnp.where(blk_of_lane >= band_blk + NBAND_B, far_right, 0.0))
        far_hi = far.astype(BF16).astype(F32)
        far_vec = jnp.where(lane < NKB_B, far_hi, jnp.where(lane < 2 * NKB_B, far - far_hi, 0.0))
        far_q = jnp.broadcast_to(far_vec, (TQ_B, LANES)).astype(BF16)
        band = slice(band_blk * BLK, (band_blk + NBAND_B) * BLK)
        for c, qc_ref in enumerate((q0_ref, q1_ref)):
            qa = jnp.concatenate([qc_ref[i * TQ_B:(i + 1) * TQ_B, :], far_q], axis=1)
            s_ref[c] = _dot_nt(qa, ka_ref[...])
            s_ref[c, :, band] = s_ref[c, :, band] + tab_ref[:, tab]

    def softmax(s_ref, p_ref):
        for c in range(2):
            s = s_ref[c]
            p_ref[c] = jnp.exp2(s - jnp.max(s, axis=-1, keepdims=True)).astype(BF16)

    def values(i, p_ref):
        res = [_dot(p_ref[c], va_ref[...]) for c in range(2)]
        c0 = 1.0 / res[0][:, LANES:LANES + 1]
        c1 = lam / res[1][:, LANES:LANES + 1]
        o = res[0][:, :LANES] * c0 - res[1][:, :LANES] * c1
        o = _rms(o, sg_ref[...]) * (1.0 - lam_init)
        o_ref[i * TQ_B:(i + 1) * TQ_B, :] = o.astype(o_ref.dtype)

    s_bufs, p_bufs = (sa_ref, sb_ref), (pa_ref, pb_ref)
    scores(0, s_bufs[0])
    scores(1, s_bufs[1])
    softmax(s_bufs[0], p_bufs[0])
    for i in range(n_tiles):
        if i + 2 < n_tiles:
            scores(i + 2, s_bufs[i % 2])
        if i + 1 < n_tiles:
            softmax(s_bufs[(i + 1) % 2], p_bufs[(i + 1) % 2])
        values(i, p_bufs[i % 2])


def _mixb(z3, far_b, qg2, kg2, lam_vecs, subln_g, tab_b, layer):
    bsz = z3.shape[0]
    lam_init = 0.8 - 0.6 * math.exp(-0.3 * layer)
    qb0 = QB0 // LANES
    kb0 = qb0 + WB // LANES
    vb0 = VB0 // LANES
    return pl.pallas_call(
        functools.partial(_mixb_kernel, lam_init=lam_init),
        grid=(bsz, HB),
        in_specs=[
            pl.BlockSpec(memory_space=pltpu.SMEM),
            pl.BlockSpec((None, SEQ, LANES), lambda b, h: (b, 0, qb0 + h)),
            pl.BlockSpec((None, SEQ, LANES), lambda b, h: (b, 0, kb0 + h)),
            pl.BlockSpec((None, SEQ, LANES), lambda b, h: (b, 0, vb0 + h)),
            pl.BlockSpec((None, 1, LANES), lambda b, h: (layer, 0, 0)),
            pl.BlockSpec((None, 1, LANES), lambda b, h: (layer, 0, 0)),
            pl.BlockSpec((None, 4, HEAD_DIM), lambda b, h: (layer, 0, 0)),
            pl.BlockSpec((None, 1, LANES), lambda b, h: (layer, 0, 0)),
            pl.BlockSpec((None, TQ_B, (NBAND_B + 2) * BLK), lambda b, h: (h, 0, 0)),
        ],
        out_specs=pl.BlockSpec((None, SEQ, LANES), lambda b, h: (b, 0, h)),
        out_shape=jax.ShapeDtypeStruct((bsz, SEQ, WB), BF16),
        scratch_shapes=[
            pltpu.VMEM((SEQ, LANES), BF16),
            pltpu.VMEM((SEQ, LANES), BF16),
            pltpu.VMEM((SEQ, 2 * LANES), BF16),
            pltpu.VMEM((SEQ, 2 * LANES), BF16),
            pltpu.VMEM((2, TQ_B, SEQ), F32),
            pltpu.VMEM((2, TQ_B, SEQ), F32),
            pltpu.VMEM((2, TQ_B, SEQ), BF16),
            pltpu.VMEM((2, TQ_B, SEQ), BF16),
        ],
        compiler_params=pltpu.CompilerParams(
            dimension_semantics=("parallel", "arbitrary"), vmem_limit_bytes=VMEM_LIMIT),
        name=f"mixb_l{layer}",
    )(far_b, z3, z3, z3, qg2, kg2, lam_vecs, subln_g, tab_b)


def _merge_kernel(x_ref, oa_ref, ob_ref, g_ref, wg_ref, bg_ref, wa_ref, wb_ref, wo_ref, o_ref):
    x = x_ref[...]
    h = (x * g_ref[...]).astype(BF16)
    gates = 1.0 / (1.0 + jnp.exp(-(_dot(h, wg_ref[...]) * _inv_rms(x) + bg_ref[...])))
    mix = (gates[:, :D_MODEL] * _dot(oa_ref[...], wa_ref[...])
           + gates[:, D_MODEL:] * _dot(ob_ref[...], wb_ref[...]))
    o_ref[...] = x + _dot(mix.astype(BF16), wo_ref[...])


def _merge(xf, oa, ob, ln_g, w_gate, b_gate, w_a, w_b, w_o, layer):
    n_tok = xf.shape[0]
    lsel = lambda i: (layer, 0, 0)
    return pl.pallas_call(
        _merge_kernel,
        grid=(n_tok // TM_PROJ,),
        in_specs=[
            pl.BlockSpec((TM_PROJ, D_MODEL), lambda i: (i, 0)),
            pl.BlockSpec((TM_PROJ, WA), lambda i: (i, 0)),
            pl.BlockSpec((TM_PROJ, WB), lambda i: (i, 0)),
            pl.BlockSpec((None, 1, D_MODEL), lsel),
            pl.BlockSpec((None, D_MODEL, 2 * D_MODEL), lsel),
            pl.BlockSpec((None, 1, 2 * D_MODEL), lsel),
            pl.BlockSpec((None, WA, D_MODEL), lsel),
            pl.BlockSpec((None, WB, D_MODEL), lsel),
            pl.BlockSpec((None, D_MODEL, D_MODEL), lsel),
        ],
        out_specs=pl.BlockSpec((TM_PROJ, D_MODEL), lambda i: (i, 0)),
        out_shape=jax.ShapeDtypeStruct((n_tok, D_MODEL), F32),
        compiler_params=pltpu.CompilerParams(
            dimension_semantics=("parallel",), vmem_limit_bytes=VMEM_LIMIT),
        name=f"merge_l{layer}",
    )(xf, oa, ob, ln_g, w_gate, b_gate, w_a, w_b, w_o)


def _ffn_kernel(xp_ref, xc_ref, xn_ref, g_ref, wup_ref, cw_ref, cb_ref, wdn_ref, o_ref,
                act_ref):
    tiles_per_seq = SEQ // TM_FFN
    t = pl.program_id(0) % tiles_per_seq
    g = g_ref[...]
    xc = xc_ref[...]
    hp = jnp.where(t == 0, 0.0, _rms(xp_ref[...], g))
    hn = jnp.where(t == tiles_per_seq - 1, 0.0, _rms(xn_ref[...], g))
    h2 = jnp.concatenate([hp, _rms(xc, g), hn], axis=0).astype(BF16)

    rows = TM_FFN + 2 * SUBLANES
    centre = slice(SUBLANES, SUBLANES + TM_FFN)

    def conv(col):
        u = _dot(h2, wup_ref[:, col:col + TC_FFN])
        w = cw_ref[:, col:col + TC_FFN]
        prev = pltpu.roll(u, 1, 0)[centre]
        nxt = pltpu.roll(u, rows - 1, 0)[centre]
        return ((cb_ref[:, col:col + TC_FFN] + prev * w[0:1]) + u[centre] * w[1:2]) + nxt * w[2:3]

    for c in range(D_FF // TC_FFN):
        cv, cg = c * TC_FFN, D_FF + c * TC_FFN
        val = conv(cv)
        gate = conv(cg)
        act = gate * (1.0 / (1.0 + jnp.exp(-gate))) * val
        act_ref[:, cv:cv + TC_FFN] = act.astype(BF16)
    o_ref[...] = xc + _dot(act_ref[...], wdn_ref[...])


def _ffn(xf, ln_g, w_up, conv_w, conv_b, w_down, layer):
    n_tok = xf.shape[0]
    lsel = lambda i: (layer, 0, 0)
    halo_per_tile = TM_FFN // SUBLANES
    n_halo = n_tok // SUBLANES
    return pl.pallas_call(
        _ffn_kernel,
        grid=(n_tok // TM_FFN,),
        in_specs=[
            pl.BlockSpec((SUBLANES, D_MODEL), lambda i: (jnp.maximum(i * halo_per_tile - 1, 0), 0)),
            pl.BlockSpec((TM_FFN, D_MODEL), lambda i: (i, 0)),
            pl.BlockSpec((SUBLANES, D_MODEL),
                         lambda i: (jnp.minimum((i + 1) * halo_per_tile, n_halo - 1), 0)),
            pl.BlockSpec((None, 1, D_MODEL), lsel),
            pl.BlockSpec((None, D_MODEL, 2 * D_FF), lsel),
            pl.BlockSpec((None, CONV_WIDTH, 2 * D_FF), lsel),
            pl.BlockSpec((None, 1, 2 * D_FF), lsel),
            pl.BlockSpec((None, D_FF, D_MODEL), lsel),
        ],
        out_specs=pl.BlockSpec((TM_FFN, D_MODEL), lambda i: (i, 0)),
        out_shape=jax.ShapeDtypeStruct((n_tok, D_MODEL), F32),
        scratch_shapes=[
            pltpu.VMEM((TM_FFN, D_FF), BF16),
        ],
        compiler_params=pltpu.CompilerParams(
            dimension_semantics=("parallel",), vmem_limit_bytes=VMEM_LIMIT),
        name=f"ffn_l{layer}",
    )(xf, xf, xf, ln_g, w_up, conv_w, conv_b, w_down)


def _t5_bucket(rel):
    half = N_BUCKETS // 2
    max_exact = half // 2
    ret = jnp.where(rel > 0, half, 0)
    n = jnp.abs(rel)
    nf = jnp.maximum(n, 1).astype(jnp.float32)
    large = max_exact + (jnp.log(nf / max_exact) / math.log(MAX_DIST / max_exact)
                         * (half - max_exact)).astype(jnp.int32)
    large = jnp.minimum(large, half - 1)
    return ret + jnp.where(n < max_exact, n, large)


def _expand_bias(bucket, tab):
    out = jnp.zeros((tab.shape[1],) + bucket.shape, F32)
    for b in range(N_BUCKETS):
        out = out + jnp.where(bucket[None] == b, tab[b][:, None, None], 0.0)
    return out


def _bias_tables(rel_bias):
    bias_a, bias_b = rel_bias[:, :HA], rel_bias[:, HA:]
    rel_np = np.arange(3 * BLK)[None, :] - BLK - np.arange(BLK)[:, None]
    base = _expand_bias(_t5_bucket(jnp.asarray(rel_np, jnp.int32)), bias_a)
    col = np.arange(3 * BLK)[None, :]
    in_win = np.abs(rel_np) <= WINDOW
    variants = []
    for valid_cols in (col >= BLK, col >= 0, col < 2 * BLK):
        variants.append(jnp.where(jnp.asarray(in_win & valid_cols)[None], base * LOG2E, NEG))
    tab_a = jnp.stack(variants).reshape(3, KV_A, HA // KV_A * BLK, 3 * BLK)
    rel_b = np.arange((NBAND_B + 2) * BLK)[None, :] - 2 * BLK - np.arange(TQ_B)[:, None]
    tab_b = _expand_bias(_t5_bucket(jnp.asarray(rel_b, jnp.int32)), bias_b) * LOG2E
    far_b = jnp.stack([bias_b[N_BUCKETS // 2 - 1], bias_b[N_BUCKETS - 1]], axis=1) * LOG2E
    return tab_a, tab_b, far_b


def kernel(x, ln1_g, w_in, qn_a, kn_a, sink, qn_b, kn_b, lam_q1, lam_k1, lam_q2, lam_k2,
           subln_g, rel_bias, w_gate, b_gate, w_a_proj, w_b_proj, w_o, ln2_g, w_up,
           conv_w, conv_b, w_down):
    bsz, seq, d = x.shape
    assert (bsz, seq, d) == (BATCH, SEQ, D_MODEL)
    tab_a, tab_b, far_b = _bias_tables(rel_bias.astype(F32))
    row = lambda a: a.astype(F32)[:, None, :]
    twice = lambda a: jnp.concatenate([a, a], axis=-1).astype(F32)[:, None, :]
    ln1, ln2, sg = row(ln1_g), row(ln2_g), row(subln_g)
    qga, kga, qgb, kgb = twice(qn_a), twice(kn_a), twice(qn_b), twice(kn_b)
    lam_vecs = jnp.stack([lam_q1, lam_k1, lam_q2, lam_k2], axis=1).astype(F32)
    bg, cb = row(b_gate), row(conv_b)
    cw = conv_w.astype(F32)
    snk = sink.astype(F32)
    w_in_b, w_gate_b, w_a_b, w_b_b, w_o_b, w_up_b, w_down_b = (
        w.astype(BF16) for w in (w_in, w_gate, w_a_proj, w_b_proj, w_o, w_up, w_down))

    xf = x.astype(F32).reshape(bsz * seq, d)
    for layer in range(DEPTH):
        z3 = _inproj(xf, ln1, w_in_b, layer).reshape(bsz, seq, IN_W)
        oa = _mixa(z3, snk, qga, kga, tab_a, layer).reshape(bsz * seq, WA)
        ob = _mixb(z3, far_b, qgb, kgb, lam_vecs, sg, tab_b, layer).reshape(bsz * seq, WB)
        xf = _merge(xf, oa, ob, ln1, w_gate_b, bg, w_a_b, w_b_b, w_o_b, layer)
        xf = _ffn(xf, ln2, w_up_b, cw, cb, w_down_b, layer)
    return xf.reshape(bsz, seq, d).astype(x.dtype)
```

```python
import functools
import math

import numpy as np
import jax
import jax.numpy as jnp
from jax import lax
from jax.experimental import pallas as pl
from jax.experimental.pallas import tpu as pltpu

D_MODEL = 1024
BATCH = 8
SEQ = 2048
DEPTH = 4
HEAD_DIM = 64
HA = 8
KV_A = 2
HB = 4
WA = HA * HEAD_DIM
WB = HB * 2 * HEAD_DIM
WINDOW = 128
BLK = 128
N_BUCKETS = 32
MAX_DIST = 128
D_FF = 2816
CONV_WIDTH = 3
EPS = 1e-6
IN_W = WA + 2 * KV_A * HEAD_DIM + 3 * WB
QB0 = WA + 2 * KV_A * HEAD_DIM
VB0 = QB0 + 2 * WB
SCALE = HEAD_DIM ** -0.5
NEG = -1e30

LANES = 128
SUBLANES = 8
VMEM_LIMIT = 48 * 1024 * 1024

TM_PROJ = 512
TM_FFN = 512
TC_FFN = 256
TQ_B = 512
NKB_B = SEQ // BLK
NBAND_B = TQ_B // BLK + 2
LOG2E = math.log2(math.e)
assert 2 * NKB_B <= LANES and SEQ // TQ_B >= 3 and MAX_DIST <= BLK

F32 = jnp.float32
BF16 = jnp.bfloat16


def _dot(a, b):
    return jnp.dot(a, b, preferred_element_type=F32)


def _dot_nt(a, b):
    return lax.dot_general(a, b, (((1,), (1,)), ((), ())), preferred_element_type=F32)


def _inv_rms(x):
    return lax.rsqrt(jnp.mean(x * x, axis=-1, keepdims=True) + EPS)


def _rms(x, g):
    return x * _inv_rms(x) * g


def _half_rms(x, g2, lo):
    x2 = x * x
    s_all = jnp.sum(x2, axis=-1, keepdims=True)
    s_lo = jnp.sum(jnp.where(lo, x2, 0.0), axis=-1, keepdims=True)
    ss = jnp.where(lo, s_lo, s_all - s_lo)
    return x * lax.rsqrt(ss * (1.0 / HEAD_DIM) + EPS) * g2


def _group_ones(width):
    r = lax.broadcasted_iota(jnp.int32, (width, width), 0) // HEAD_DIM
    c = lax.broadcasted_iota(jnp.int32, (width, width), 1) // HEAD_DIM
    return jnp.where(r == c, 1.0, 0.0).astype(BF16)


def _head_rms(x, g, ones_bd):
    x2 = x * x
    hi = x2.astype(BF16)
    lo = (x2 - hi.astype(F32)).astype(BF16)
    ss = _dot(hi, ones_bd) + _dot(lo, ones_bd)
    return x * lax.rsqrt(ss * (1.0 / HEAD_DIM) + EPS) * g


def _lo_mask():
    return lax.broadcasted_iota(jnp.int32, (1, LANES), 1) < HEAD_DIM


def _inproj_kernel(x_ref, g_ref, w_ref, z_ref):
    x = x_ref[...]
    z_ref[...] = _dot((x * g_ref[...]).astype(BF16), w_ref[...]) * _inv_rms(x)


def _inproj(xf, ln_g, w_in, layer):
    n_tok = xf.shape[0]
    return pl.pallas_call(
        _inproj_kernel,
        grid=(n_tok // TM_PROJ,),
        in_specs=[
            pl.BlockSpec((TM_PROJ, D_MODEL), lambda i: (i, 0)),
            pl.BlockSpec((None, 1, D_MODEL), lambda i: (layer, 0, 0)),
            pl.BlockSpec((None, D_MODEL, IN_W), lambda i: (layer, 0, 0)),
        ],
        out_specs=pl.BlockSpec((TM_PROJ, IN_W), lambda i: (i, 0)),
        out_shape=jax.ShapeDtypeStruct((n_tok, IN_W), F32),
        compiler_params=pltpu.CompilerParams(
            dimension_semantics=("parallel",), vmem_limit_bytes=VMEM_LIMIT),
        name=f"inproj_l{layer}",
    )(xf, ln_g, w_in)


def _mixa_kernel(sink_ref, q_ref, kv_ref, qg_ref, kg_ref, bias_ref, o_ref,
                 qlo_ref, qhi_ref, kk_ref, vv_ref, sa_ref, sb_ref, pa_ref, pb_ref, *, layer):
    nb = SEQ // BLK
    group = HA // KV_A
    lo = _lo_mask()

    kv = kv_ref[...]
    k = _head_rms(kv[:, :LANES], kg_ref[...], _group_ones(LANES))
    v = kv[:, LANES:]
    kr = pltpu.roll(k, HEAD_DIM, 1)
    vr = pltpu.roll(v, HEAD_DIM, 1)
    for j in range(KV_A):
        for edge in (slice(0, BLK), slice(SEQ + BLK, SEQ + 2 * BLK)):
            kk_ref[j, edge, :] = jnp.zeros((BLK, LANES), BF16)
            vv_ref[j, edge, :] = jnp.zeros((BLK, 2 * LANES), BF16)
        vv_ref[j, BLK:BLK + SEQ, LANES:] = jnp.ones((SEQ, LANES), BF16)
    kk_ref[0, BLK:BLK + SEQ, :] = jnp.where(lo, k, kr).astype(BF16)
    kk_ref[1, BLK:BLK + SEQ, :] = jnp.where(lo, kr, k).astype(BF16)
    vv_ref[0, BLK:BLK + SEQ, :LANES] = jnp.where(lo, v, vr).astype(BF16)
    vv_ref[1, BLK:BLK + SEQ, :LANES] = jnp.where(lo, vr, v).astype(BF16)
    ones_pair = _group_ones(2 * LANES)
    lo_pair = jnp.concatenate([lo, lo], axis=1)
    qg_pair = jnp.concatenate([qg_ref[...], qg_ref[...]], axis=1)
    for j in range(KV_A):
        pair = slice(2 * LANES * j, 2 * LANES * (j + 1))
        qn = _head_rms(q_ref[:, pair], qg_pair, ones_pair) * (SCALE * LOG2E)
        qlo_ref[:, pair] = jnp.where(lo_pair, qn, 0.0).astype(BF16)
        qhi_ref[:, pair] = jnp.where(lo_pair, 0.0, qn).astype(BF16)

    head_row = lax.broadcasted_iota(jnp.int32, (group * BLK, 1), 0) // BLK
    sinks = []
    for j in range(KV_A):
        snk = jnp.zeros((group * BLK, 1), F32)
        for g in range(group):
            snk = jnp.where(head_row == g, sink_ref[layer, group * j + g] * LOG2E, snk)
        sinks.append(snk)

    def cols(j):
        return [slice(LANES * (2 * j + t), LANES * (2 * j + t + 1)) for t in range(2)]

    def scores(n, s_ref):
        variant = 0 if n == 0 else (2 if n == nb - 1 else 1)
        rows = slice(n * BLK, (n + 1) * BLK)
        window = slice(n * BLK, (n + 3) * BLK)
        for j in range(KV_A):
            qs = jnp.concatenate(
                [ref[rows, c] for c in cols(j) for ref in (qlo_ref, qhi_ref)], axis=0)
            s_ref[j] = _dot_nt(qs, kk_ref[j, window, :]) + bias_ref[variant, j]

    def softmax(s_ref, p_ref):
        sink_w = []
        for j in range(KV_A):
            s = s_ref[j]
            mx = jnp.maximum(jnp.max(s, axis=-1, keepdims=True), sinks[j])
            p_ref[j] = jnp.exp2(s - mx).astype(BF16)
            sink_w.append(jnp.exp2(sinks[j] - mx))
        return sink_w

    def values(n, p_ref, sink_w):
        rows = slice(n * BLK, (n + 1) * BLK)
        window = slice(n * BLK, (n + 3) * BLK)
        for j in range(KV_A):
            res = _dot(p_ref[j], vv_ref[j, window, :])
            o = res[:, :LANES] * (1.0 / (res[:, LANES:LANES + 1] + sink_w[j]))
            for t, c in enumerate(cols(j)):
                o_ref[rows, c] = jnp.where(
                    lo, o[2 * t * BLK:(2 * t + 1) * BLK], o[(2 * t + 1) * BLK:(2 * t + 2) * BLK]
                ).astype(o_ref.dtype)

    s_bufs, p_bufs = (sa_ref, sb_ref), (pa_ref, pb_ref)
    scores(0, s_bufs[0])
    scores(1, s_bufs[1])
    sink_w = {0: softmax(s_bufs[0], p_bufs[0])}
    for n in range(nb):
        if n + 2 < nb:
            scores(n + 2, s_bufs[n % 2])
        if n + 1 < nb:
            sink_w[n + 1] = softmax(s_bufs[(n + 1) % 2], p_bufs[(n + 1) % 2])
        values(n, p_bufs[n % 2], sink_w.pop(n))


def _mixa(z3, sink, qg2, kg2, tab_a, layer):
    bsz = z3.shape[0]
    return pl.pallas_call(
        functools.partial(_mixa_kernel, layer=layer),
        grid=(bsz,),
        in_specs=[
            pl.BlockSpec(memory_space=pltpu.SMEM),
            pl.BlockSpec((None, SEQ, WA), lambda b: (b, 0, 0)),
            pl.BlockSpec((None, SEQ, 2 * LANES), lambda b: (b, 0, WA // (2 * LANES))),
            pl.BlockSpec((None, 1, LANES), lambda b: (layer, 0, 0)),
            pl.BlockSpec((None, 1, LANES), lambda b: (layer, 0, 0)),
            pl.BlockSpec((3, KV_A, HA // KV_A * BLK, 3 * BLK), lambda b: (0, 0, 0, 0)),
        ],
        out_specs=pl.BlockSpec((None, SEQ, WA), lambda b: (b, 0, 0)),
        out_shape=jax.ShapeDtypeStruct((bsz, SEQ, WA), BF16),
        scratch_shapes=[
            pltpu.VMEM((SEQ, WA), BF16),
            pltpu.VMEM((SEQ, WA), BF16),
            pltpu.VMEM((KV_A, SEQ + 2 * BLK, LANES), BF16),
            pltpu.VMEM((KV_A, SEQ + 2 * BLK, 2 * LANES), BF16),
            pltpu.VMEM((KV_A, HA // KV_A * BLK, 3 * BLK), F32),
            pltpu.VMEM((KV_A, HA // KV_A * BLK, 3 * BLK), F32),
            pltpu.VMEM((KV_A, HA // KV_A * BLK, 3 * BLK), BF16),
            pltpu.VMEM((KV_A, HA // KV_A * BLK, 3 * BLK), BF16),
        ],
        compiler_params=pltpu.CompilerParams(
            dimension_semantics=("parallel",), vmem_limit_bytes=VMEM_LIMIT),
        name=f"mixa_l{layer}",
    )(sink, z3, z3, qg2, kg2, tab_a)


def _mixb_kernel(far_ref, q_ref, k_ref, v_ref, qg_ref, kg_ref, lam_ref, sg_ref, tab_ref, o_ref,
                 q0_ref, q1_ref, ka_ref, va_ref, sa_ref, sb_ref, pa_ref, pb_ref, *, lam_init):
    head = pl.program_id(1)
    lo = _lo_mask()
    lane = lax.broadcasted_iota(jnp.int32, (1, LANES), 1)
    ka_ref[:, :LANES] = _half_rms(k_ref[...], kg_ref[...], lo).astype(BF16)
    key_blk = lax.broadcasted_iota(jnp.int32, (SEQ, LANES), 0) // BLK
    key_lane = lax.broadcasted_iota(jnp.int32, (SEQ, LANES), 1)
    onehot = (key_lane < 2 * NKB_B) & ((key_lane % NKB_B) == key_blk)
    ka_ref[:, LANES:] = jnp.where(onehot, 1.0, 0.0).astype(BF16)
    va_ref[:, :LANES] = v_ref[...].astype(BF16)
    va_ref[:, LANES:] = jnp.ones((SEQ, LANES), BF16)
    qn = _half_rms(q_ref[...], qg_ref[...], lo) * (SCALE * LOG2E)
    q0_ref[...] = jnp.where(lo, qn, 0.0).astype(BF16)
    q1_ref[...] = jnp.where(lo, 0.0, qn).astype(BF16)
    lv = lam_ref[...]
    lam = (jnp.exp(jnp.sum(lv[0:1] * lv[1:2], axis=-1, keepdims=True))
           - jnp.exp(jnp.sum(lv[2:3] * lv[3:4], axis=-1, keepdims=True)) + lam_init)
    far_left, far_right = far_ref[head, 0], far_ref[head, 1]
    n_tiles = SEQ // TQ_B
    blocks_per_tile = TQ_B // BLK
    blk_of_lane = lane % NKB_B

    def scores(i, s_ref):
        band_blk = min(max(blocks_per_tile * i - 1, 0), NKB_B - NBAND_B)
        lead = blocks_per_tile * i - band_blk
        tab = slice((2 - lead) * BLK, (2 - lead + NBAND_B) * BLK)
        far = jnp.where(blk_of_lane < band_blk, far_left,
                        jnp.where(blk_of_lane >= band_blk + NBAND_B, far_right, 0.0))
        far_hi = far.astype(BF16).astype(F32)
        far_vec = jnp.where(lane < NKB_B, far_hi, jnp.where(lane < 2 * NKB_B, far - far_hi, 0.0))
        far_q = jnp.broadcast_to(far_vec, (TQ_B, LANES)).astype(BF16)
        band = slice(band_blk * BLK, (band_blk + NBAND_B) * BLK)
        for c, qc_ref in enumerate((q0_ref, q1_ref)):
            qa = jnp.concatenate([qc_ref[i * TQ_B:(i + 1) * TQ_B, :], far_q], axis=1)
            s_ref[c] = _dot_nt(qa, ka_ref[...])
            s_ref[c, :, band] = s_ref[c, :, band] + tab_ref[:, tab]

    def softmax(s_ref, p_ref):
        for c in range(2):
            s = s_ref[c]
            p_ref[c] = jnp.exp2(s - jnp.max(s, axis=-1, keepdims=True)).astype(BF16)

    def values(i, p_ref):
        res = [_dot(p_ref[c], va_ref[...]) for c in range(2)]
        c0 = 1.0 / res[0][:, LANES:LANES + 1]
        c1 = lam / res[1][:, LANES:LANES + 1]
        o = res[0][:, :LANES] * c0 - res[1][:, :LANES] * c1
        o = _rms(o, sg_ref[...]) * (1.0 - lam_init)
        o_ref[i * TQ_B:(i + 1) * TQ_B, :] = o.astype(o_ref.dtype)

    s_bufs, p_bufs = (sa_ref, sb_ref), (pa_ref, pb_ref)
    scores(0, s_bufs[0])
    scores(1, s_bufs[1])
    softmax(s_bufs[0], p_bufs[0])
    for i in range(n_tiles):
        if i + 2 < n_tiles:
            scores(i + 2, s_bufs[i % 2])
        if i + 1 < n_tiles:
            softmax(s_bufs[(i + 1) % 2], p_bufs[(i + 1) % 2])
        values(i, p_bufs[i % 2])


def _mixb(z3, far_b, qg2, kg2, lam_vecs, subln_g, tab_b, layer):
    bsz = z3.shape[0]
    lam_init = 0.8 - 0.6 * math.exp(-0.3 * layer)
    qb0 = QB0 // LANES
    kb0 = qb0 + WB // LANES
    vb0 = VB0 // LANES
    return pl.pallas_call(
        functools.partial(_mixb_kernel, lam_init=lam_init),
        grid=(bsz, HB),
        in_specs=[
            pl.BlockSpec(memory_space=pltpu.SMEM),
            pl.BlockSpec((None, SEQ, LANES), lambda b, h: (b, 0, qb0 + h)),
            pl.BlockSpec((None, SEQ, LANES), lambda b, h: (b, 0, kb0 + h)),
            pl.BlockSpec((None, SEQ, LANES), lambda b, h: (b, 0, vb0 + h)),
            pl.BlockSpec((None, 1, LANES), lambda b, h: (layer, 0, 0)),
            pl.BlockSpec((None, 1, LANES), lambda b, h: (layer, 0, 0)),
            pl.BlockSpec((None, 4, HEAD_DIM), lambda b, h: (layer, 0, 0)),
            pl.BlockSpec((None, 1, LANES), lambda b, h: (layer, 0, 0)),
            pl.BlockSpec((None, TQ_B, (NBAND_B + 2) * BLK), lambda b, h: (h, 0, 0)),
        ],
        out_specs=pl.BlockSpec((None, SEQ, LANES), lambda b, h: (b, 0, h)),
        out_shape=jax.ShapeDtypeStruct((bsz, SEQ, WB), BF16),
        scratch_shapes=[
            pltpu.VMEM((SEQ, LANES), BF16),
            pltpu.VMEM((SEQ, LANES), BF16),
            pltpu.VMEM((SEQ, 2 * LANES), BF16),
            pltpu.VMEM((SEQ, 2 * LANES), BF16),
            pltpu.VMEM((2, TQ_B, SEQ), F32),
            pltpu.VMEM((2, TQ_B, SEQ), F32),
            pltpu.VMEM((2, TQ_B, SEQ), BF16),
            pltpu.VMEM((2, TQ_B, SEQ), BF16),
        ],
        compiler_params=pltpu.CompilerParams(
            dimension_semantics=("parallel", "arbitrary"), vmem_limit_bytes=VMEM_LIMIT),
        name=f"mixb_l{layer}",
    )(far_b, z3, z3, z3, qg2, kg2, lam_vecs, subln_g, tab_b)


def _merge_kernel(x_ref, oa_ref, ob_ref, g_ref, wg_ref, bg_ref, wa_ref, wb_ref, wo_ref, o_ref):
    x = x_ref[...]
    h = (x * g_ref[...]).astype(BF16)
    pre = _dot(h, wg_ref[...]) * _inv_rms(x) + bg_ref[...]
    gates = 1.0 / (1.0 + jnp.exp2(pre * -LOG2E))
    mix = (gates[:, :D_MODEL] * _dot(oa_ref[...], wa_ref[...])
           + gates[:, D_MODEL:] * _dot(ob_ref[...], wb_ref[...]))
    o_ref[...] = x + _dot(mix.astype(BF16), wo_ref[...])


def _merge(xf, oa, ob, ln_g, w_gate, b_gate, w_a, w_b, w_o, layer):
    n_tok = xf.shape[0]
    lsel = lambda i: (layer, 0, 0)
    return pl.pallas_call(
        _merge_kernel,
        grid=(n_tok // TM_PROJ,),
        in_specs=[
            pl.BlockSpec((TM_PROJ, D_MODEL), lambda i: (i, 0)),
            pl.BlockSpec((TM_PROJ, WA), lambda i: (i, 0)),
            pl.BlockSpec((TM_PROJ, WB), lambda i: (i, 0)),
            pl.BlockSpec((None, 1, D_MODEL), lsel),
            pl.BlockSpec((None, D_MODEL, 2 * D_MODEL), lsel),
            pl.BlockSpec((None, 1, 2 * D_MODEL), lsel),
            pl.BlockSpec((None, WA, D_MODEL), lsel),
            pl.BlockSpec((None, WB, D_MODEL), lsel),
            pl.BlockSpec((None, D_MODEL, D_MODEL), lsel),
        ],
        out_specs=pl.BlockSpec((TM_PROJ, D_MODEL), lambda i: (i, 0)),
        out_shape=jax.ShapeDtypeStruct((n_tok, D_MODEL), F32),
        compiler_params=pltpu.CompilerParams(
            dimension_semantics=("parallel",), vmem_limit_bytes=VMEM_LIMIT),
        name=f"merge_l{layer}",
    )(xf, oa, ob, ln_g, w_gate, b_gate, w_a, w_b, w_o)


def _ffn_kernel(xp_ref, xc_ref, xn_ref, g_ref, wup_ref, cw_ref, cb_ref, wdn_ref, o_ref,
                act_ref):
    tiles_per_seq = SEQ // TM_FFN
    t = pl.program_id(0) % tiles_per_seq
    g = g_ref[...]
    xc = xc_ref[...]
    hp = jnp.where(t == 0, 0.0, _rms(xp_ref[...], g))
    hn = jnp.where(t == tiles_per_seq - 1, 0.0, _rms(xn_ref[...], g))
    h2 = jnp.concatenate([hp, _rms(xc, g), hn], axis=0).astype(BF16)

    rows = TM_FFN + 2 * SUBLANES
    centre = slice(SUBLANES, SUBLANES + TM_FFN)

    def conv(col):
        u = _dot(h2, wup_ref[:, col:col + TC_FFN])
        w = cw_ref[:, col:col + TC_FFN]
        prev = pltpu.roll(u, 1, 0)[centre]
        nxt = pltpu.roll(u, rows - 1, 0)[centre]
        return ((cb_ref[:, col:col + TC_FFN] + prev * w[0:1]) + u[centre] * w[1:2]) + nxt * w[2:3]

    for c in range(D_FF // TC_FFN):
        cv, cg = c * TC_FFN, D_FF + c * TC_FFN
        val = conv(cv)
        gate = conv(cg)
        act = gate * (1.0 / (1.0 + jnp.exp2(gate * -LOG2E))) * val
        act_ref[:, cv:cv + TC_FFN] = act.astype(BF16)
    o_ref[...] = xc + _dot(act_ref[...], wdn_ref[...])


def _ffn(xf, ln_g, w_up, conv_w, conv_b, w_down, layer):
    n_tok = xf.shape[0]
    lsel = lambda i: (layer, 0, 0)
    halo_per_tile = TM_FFN // SUBLANES
    n_halo = n_tok // SUBLANES
    return pl.pallas_call(
        _ffn_kernel,
        grid=(n_tok // TM_FFN,),
        in_specs=[
            pl.BlockSpec((SUBLANES, D_MODEL), lambda i: (jnp.maximum(i * halo_per_tile - 1, 0), 0)),
            pl.BlockSpec((TM_FFN, D_MODEL), lambda i: (i, 0)),
            pl.BlockSpec((SUBLANES, D_MODEL),
                         lambda i: (jnp.minimum((i + 1) * halo_per_tile, n_halo - 1), 0)),
            pl.BlockSpec((None, 1, D_MODEL), lsel),
            pl.BlockSpec((None, D_MODEL, 2 * D_FF), lsel),
            pl.BlockSpec((None, CONV_WIDTH, 2 * D_FF), lsel),
            pl.BlockSpec((None, 1, 2 * D_FF), lsel),
            pl.BlockSpec((None, D_FF, D_MODEL), lsel),
        ],
        out_specs=pl.BlockSpec((TM_FFN, D_MODEL), lambda i: (i, 0)),
        out_shape=jax.ShapeDtypeStruct((n_tok, D_MODEL), F32),
        scratch_shapes=[
            pltpu.VMEM((TM_FFN, D_FF), BF16),
        ],
        compiler_params=pltpu.CompilerParams(
            dimension_semantics=("parallel",), vmem_limit_bytes=VMEM_LIMIT),
        name=f"ffn_l{layer}",
    )(xf, xf, xf, ln_g, w_up, conv_w, conv_b, w_down)


def _t5_bucket(rel):
    half = N_BUCKETS // 2
    max_exact = half // 2
    ret = jnp.where(rel > 0, half, 0)
    n = jnp.abs(rel)
    nf = jnp.maximum(n, 1).astype(jnp.float32)
    large = max_exact + (jnp.log(nf / max_exact) / math.log(MAX_DIST / max_exact)
                         * (half - max_exact)).astype(jnp.int32)
    large = jnp.minimum(large, half - 1)
    return ret + jnp.where(n < max_exact, n, large)


def _expand_bias(bucket, tab):
    out = jnp.zeros((tab.shape[1],) + bucket.shape, F32)
    for b in range(N_BUCKETS):
        out = out + jnp.where(bucket[None] == b, tab[b][:, None, None], 0.0)
    return out


def _bias_tables(rel_bias):
    bias_a, bias_b = rel_bias[:, :HA], rel_bias[:, HA:]
    rel_np = np.arange(3 * BLK)[None, :] - BLK - np.arange(BLK)[:, None]
    base = _expand_bias(_t5_bucket(jnp.asarray(rel_np, jnp.int32)), bias_a)
    col = np.arange(3 * BLK)[None, :]
    in_win = np.abs(rel_np) <= WINDOW
    variants = []
    for valid_cols in (col >= BLK, col >= 0, col < 2 * BLK):
        variants.append(jnp.where(jnp.asarray(in_win & valid_cols)[None], base * LOG2E, NEG))
    tab_a = jnp.stack(variants).reshape(3, KV_A, HA // KV_A * BLK, 3 * BLK)
    rel_b = np.arange((NBAND_B + 2) * BLK)[None, :] - 2 * BLK - np.arange(TQ_B)[:, None]
    tab_b = _expand_bias(_t5_bucket(jnp.asarray(rel_b, jnp.int32)), bias_b) * LOG2E
    far_b = jnp.stack([bias_b[N_BUCKETS // 2 - 1], bias_b[N_BUCKETS - 1]], axis=1) * LOG2E
    return tab_a, tab_b, far_b


def kernel(x, ln1_g, w_in, qn_a, kn_a, sink, qn_b, kn_b, lam_q1, lam_k1, lam_q2, lam_k2,
           subln_g, rel_bias, w_gate, b_gate, w_a_proj, w_b_proj, w_o, ln2_g, w_up,
           conv_w, conv_b, w_down):
    bsz, seq, d = x.shape
    assert (bsz, seq, d) == (BATCH, SEQ, D_MODEL)
    tab_a, tab_b, far_b = _bias_tables(rel_bias.astype(F32))
    row = lambda a: a.astype(F32)[:, None, :]
    twice = lambda a: jnp.concatenate([a, a], axis=-1).astype(F32)[:, None, :]
    ln1, ln2, sg = row(ln1_g), row(ln2_g), row(subln_g)
    qga, kga, qgb, kgb = twice(qn_a), twice(kn_a), twice(qn_b), twice(kn_b)
    lam_vecs = jnp.stack([lam_q1, lam_k1, lam_q2, lam_k2], axis=1).astype(F32)
    bg, cb = row(b_gate), row(conv_b)
    cw = conv_w.astype(F32)
    snk = sink.astype(F32)
    w_in_b, w_gate_b, w_a_b, w_b_b, w_o_b, w_up_b, w_down_b = (
        w.astype(BF16) for w in (w_in, w_gate, w_a_proj, w_b_proj, w_o, w_up, w_down))

    xf = x.astype(F32).reshape(bsz * seq, d)
    for layer in range(DEPTH):
        z3 = _inproj(xf, ln1, w_in_b, layer).reshape(bsz, seq, IN_W)
        oa = _mixa(z3, snk, qga, kga, tab_a, layer).reshape(bsz * seq, WA)
        ob = _mixb(z3, far_b, qgb, kgb, lam_vecs, sg, tab_b, layer).reshape(bsz * seq, WB)
        xf = _merge(xf, oa, ob, ln1, w_gate_b, bg, w_a_b, w_b_b, w_o_b, layer)
        xf = _ffn(xf, ln2, w_up_b, cw, cb, w_down_b, layer)
    return xf.reshape(bsz, seq, d).astype(x.dtype)
```

```python
import functools
import math

import numpy as np
import jax
import jax.numpy as jnp
from jax import lax
from jax.experimental import pallas as pl
from jax.experimental.pallas import tpu as pltpu

D_MODEL = 1024
BATCH = 8
SEQ = 2048
DEPTH = 4
HEAD_DIM = 64
HA = 8
KV_A = 2
HB = 4
WA = HA * HEAD_DIM
WB = HB * 2 * HEAD_DIM
WINDOW = 128
BLK = 128
N_BUCKETS = 32
MAX_DIST = 128
D_FF = 2816
CONV_WIDTH = 3
EPS = 1e-6
IN_W = WA + 2 * KV_A * HEAD_DIM + 3 * WB
QB0 = WA + 2 * KV_A * HEAD_DIM
VB0 = QB0 + 2 * WB
SCALE = HEAD_DIM ** -0.5
NEG = -1e30

LANES = 128
SUBLANES = 8
VMEM_LIMIT = 48 * 1024 * 1024

TM_PROJ = 512
TM_FFN = 512
TC_FFN = 256
TQ_B = 512
NKB_B = SEQ // BLK
NBAND_B = TQ_B // BLK + 2
LOG2E = math.log2(math.e)
assert 2 * NKB_B <= LANES and SEQ // TQ_B >= 3 and MAX_DIST <= BLK

F32 = jnp.float32
BF16 = jnp.bfloat16


def _dot(a, b):
    return jnp.dot(a, b, preferred_element_type=F32)


def _dot_nt(a, b):
    return lax.dot_general(a, b, (((1,), (1,)), ((), ())), preferred_element_type=F32)


def _inv_rms(x):
    return lax.rsqrt(jnp.mean(x * x, axis=-1, keepdims=True) + EPS)


def _rms(x, g):
    return x * _inv_rms(x) * g


def _half_rms(x, g2, lo):
    x2 = x * x
    s_all = jnp.sum(x2, axis=-1, keepdims=True)
    s_lo = jnp.sum(jnp.where(lo, x2, 0.0), axis=-1, keepdims=True)
    ss = jnp.where(lo, s_lo, s_all - s_lo)
    return x * lax.rsqrt(ss * (1.0 / HEAD_DIM) + EPS) * g2


def _group_ones(width):
    r = lax.broadcasted_iota(jnp.int32, (width, width), 0) // HEAD_DIM
    c = lax.broadcasted_iota(jnp.int32, (width, width), 1) // HEAD_DIM
    return jnp.where(r == c, 1.0, 0.0).astype(BF16)


def _head_rms(x, g, ones_bd):
    x2 = x * x
    hi = x2.astype(BF16)
    lo = (x2 - hi.astype(F32)).astype(BF16)
    ss = _dot(hi, ones_bd) + _dot(lo, ones_bd)
    return x * lax.rsqrt(ss * (1.0 / HEAD_DIM) + EPS) * g


def _lo_mask():
    return lax.broadcasted_iota(jnp.int32, (1, LANES), 1) < HEAD_DIM


def _inproj_kernel(x_ref, g_ref, w_ref, z_ref):
    x = x_ref[...]
    z_ref[...] = _dot((x * g_ref[...]).astype(BF16), w_ref[...]) * _inv_rms(x)


def _inproj(xf, ln_g, w_in, layer):
    n_tok = xf.shape[0]
    return pl.pallas_call(
        _inproj_kernel,
        grid=(n_tok // TM_PROJ,),
        in_specs=[
            pl.BlockSpec((TM_PROJ, D_MODEL), lambda i: (i, 0)),
            pl.BlockSpec((None, 1, D_MODEL), lambda i: (layer, 0, 0)),
            pl.BlockSpec((None, D_MODEL, IN_W), lambda i: (layer, 0, 0)),
        ],
        out_specs=pl.BlockSpec((TM_PROJ, IN_W), lambda i: (i, 0)),
        out_shape=jax.ShapeDtypeStruct((n_tok, IN_W), F32),
        compiler_params=pltpu.CompilerParams(
            dimension_semantics=("parallel",), vmem_limit_bytes=VMEM_LIMIT),
        name=f"inproj_l{layer}",
    )(xf, ln_g, w_in)


def _mixa_kernel(sink_ref, q_ref, kv_ref, qg_ref, kg_ref, bias_ref, o_ref,
                 qlo_ref, qhi_ref, kk_ref, vv_ref, sa_ref, sb_ref, pa_ref, pb_ref, *, layer):
    nb = SEQ // BLK
    group = HA // KV_A
    lo = _lo_mask()

    kv = kv_ref[...]
    k = _head_rms(kv[:, :LANES], kg_ref[...], _group_ones(LANES))
    v = kv[:, LANES:]
    kr = pltpu.roll(k, HEAD_DIM, 1)
    vr = pltpu.roll(v, HEAD_DIM, 1)
    for j in range(KV_A):
        for edge in (slice(0, BLK), slice(SEQ + BLK, SEQ + 2 * BLK)):
            kk_ref[j, edge, :] = jnp.zeros((BLK, LANES), BF16)
            vv_ref[j, edge, :] = jnp.zeros((BLK, 2 * LANES), BF16)
        vv_ref[j, BLK:BLK + SEQ, LANES:] = jnp.ones((SEQ, LANES), BF16)
    kk_ref[0, BLK:BLK + SEQ, :] = jnp.where(lo, k, kr).astype(BF16)
    kk_ref[1, BLK:BLK + SEQ, :] = jnp.where(lo, kr, k).astype(BF16)
    vv_ref[0, BLK:BLK + SEQ, :LANES] = jnp.where(lo, v, vr).astype(BF16)
    vv_ref[1, BLK:BLK + SEQ, :LANES] = jnp.where(lo, vr, v).astype(BF16)
    ones_pair = _group_ones(2 * LANES)
    lo_pair = jnp.concatenate([lo, lo], axis=1)
    qg_pair = jnp.concatenate([qg_ref[...], qg_ref[...]], axis=1)
    for j in range(KV_A):
        pair = slice(2 * LANES * j, 2 * LANES * (j + 1))
        qn = _head_rms(q_ref[:, pair], qg_pair, ones_pair) * (SCALE * LOG2E)
        qlo_ref[:, pair] = jnp.where(lo_pair, qn, 0.0).astype(BF16)
        qhi_ref[:, pair] = jnp.where(lo_pair, 0.0, qn).astype(BF16)

    head_row = lax.broadcasted_iota(jnp.int32, (group * BLK, 1), 0) // BLK
    sinks = []
    for j in range(KV_A):
        snk = jnp.zeros((group * BLK, 1), F32)
        for g in range(group):
            snk = jnp.where(head_row == g, sink_ref[layer, group * j + g] * LOG2E, snk)
        sinks.append(snk)

    def cols(j):
        return [slice(LANES * (2 * j + t), LANES * (2 * j + t + 1)) for t in range(2)]

    def scores(n, s_ref):
        variant = 0 if n == 0 else (2 if n == nb - 1 else 1)
        rows = slice(n * BLK, (n + 1) * BLK)
        window = slice(n * BLK, (n + 3) * BLK)
        for j in range(KV_A):
            qs = jnp.concatenate(
                [ref[rows, c] for c in cols(j) for ref in (qlo_ref, qhi_ref)], axis=0)
            s_ref[j] = _dot_nt(qs, kk_ref[j, window, :]) + bias_ref[variant, j]

    def softmax(s_ref, p_ref):
        sink_w = []
        for j in range(KV_A):
            s = s_ref[j]
            mx = jnp.maximum(jnp.max(s, axis=-1, keepdims=True), sinks[j])
            p_ref[j] = jnp.exp2(s - mx).astype(BF16)
            sink_w.append(jnp.exp2(sinks[j] - mx))
        return sink_w

    def values(n, p_ref, sink_w):
        rows = slice(n * BLK, (n + 1) * BLK)
        window = slice(n * BLK, (n + 3) * BLK)
        for j in range(KV_A):
            res = _dot(p_ref[j], vv_ref[j, window, :])
            o = res[:, :LANES] * (1.0 / (res[:, LANES:LANES + 1] + sink_w[j]))
            for t, c in enumerate(cols(j)):
                o_ref[rows, c] = jnp.where(
                    lo, o[2 * t * BLK:(2 * t + 1) * BLK], o[(2 * t + 1) * BLK:(2 * t + 2) * BLK]
                ).astype(o_ref.dtype)

    s_bufs, p_bufs = (sa_ref, sb_ref), (pa_ref, pb_ref)
    scores(0, s_bufs[0])
    scores(1, s_bufs[1])
    sink_w = {0: softmax(s_bufs[0], p_bufs[0])}
    for n in range(nb):
        if n + 2 < nb:
            scores(n + 2, s_bufs[n % 2])
        if n + 1 < nb:
            sink_w[n + 1] = softmax(s_bufs[(n + 1) % 2], p_bufs[(n + 1) % 2])
        values(n, p_bufs[n % 2], sink_w.pop(n))


def _mixa(z3, sink, qg2, kg2, tab_a, layer):
    bsz = z3.shape[0]
    return pl.pallas_call(
        functools.partial(_mixa_kernel, layer=layer),
        grid=(bsz,),
        in_specs=[
            pl.BlockSpec(memory_space=pltpu.SMEM),
            pl.BlockSpec((None, SEQ, WA), lambda b: (b, 0, 0)),
            pl.BlockSpec((None, SEQ, 2 * LANES), lambda b: (b, 0, WA // (2 * LANES))),
            pl.BlockSpec((None, 1, LANES), lambda b: (layer, 0, 0)),
            pl.BlockSpec((None, 1, LANES), lambda b: (layer, 0, 0)),
            pl.BlockSpec((3, KV_A, HA // KV_A * BLK, 3 * BLK), lambda b: (0, 0, 0, 0)),
        ],
        out_specs=pl.BlockSpec((None, SEQ, WA), lambda b: (b, 0, 0)),
        out_shape=jax.ShapeDtypeStruct((bsz, SEQ, WA), BF16),
        scratch_shapes=[
            pltpu.VMEM((SEQ, WA), BF16),
            pltpu.VMEM((SEQ, WA), BF16),
            pltpu.VMEM((KV_A, SEQ + 2 * BLK, LANES), BF16),
            pltpu.VMEM((KV_A, SEQ + 2 * BLK, 2 * LANES), BF16),
            pltpu.VMEM((KV_A, HA // KV_A * BLK, 3 * BLK), F32),
            pltpu.VMEM((KV_A, HA // KV_A * BLK, 3 * BLK), F32),
            pltpu.VMEM((KV_A, HA // KV_A * BLK, 3 * BLK), BF16),
            pltpu.VMEM((KV_A, HA // KV_A * BLK, 3 * BLK), BF16),
        ],
        compiler_params=pltpu.CompilerParams(
            dimension_semantics=("parallel",), vmem_limit_bytes=VMEM_LIMIT),
        name=f"mixa_l{layer}",
    )(sink, z3, z3, qg2, kg2, tab_a)


def _mixb_kernel(far_ref, q_ref, k_ref, v_ref, qg_ref, kg_ref, lam_ref, sg_ref, tab_ref, o_ref,
                 q0_ref, q1_ref, ka_ref, va_ref, sa_ref, sb_ref, pa_ref, pb_ref, *, lam_init):
    head = pl.program_id(1)
    lo = _lo_mask()
    lane = lax.broadcasted_iota(jnp.int32, (1, LANES), 1)
    ka_ref[:, :LANES] = _half_rms(k_ref[...], kg_ref[...], lo).astype(BF16)
    key_blk = lax.broadcasted_iota(jnp.int32, (SEQ, LANES), 0) // BLK
    key_lane = lax.broadcasted_iota(jnp.int32, (SEQ, LANES), 1)
    onehot = (key_lane < 2 * NKB_B) & ((key_lane % NKB_B) == key_blk)
    ka_ref[:, LANES:] = jnp.where(onehot, 1.0, 0.0).astype(BF16)
    va_ref[:, :LANES] = v_ref[...].astype(BF16)
    va_ref[:, LANES:] = jnp.ones((SEQ, LANES), BF16)
    qn = _half_rms(q_ref[...], qg_ref[...], lo) * (SCALE * LOG2E)
    q0_ref[...] = jnp.where(lo, qn, 0.0).astype(BF16)
    q1_ref[...] = jnp.where(lo, 0.0, qn).astype(BF16)
    lv = lam_ref[...]
    lam = (jnp.exp(jnp.sum(lv[0:1] * lv[1:2], axis=-1, keepdims=True))
           - jnp.exp(jnp.sum(lv[2:3] * lv[3:4], axis=-1, keepdims=True)) + lam_init)
    far_left, far_right = far_ref[head, 0], far_ref[head, 1]
    n_tiles = SEQ // TQ_B
    blocks_per_tile = TQ_B // BLK
    blk_of_lane = lane % NKB_B

    def scores(i, s_ref):
        band_blk = min(max(blocks_per_tile * i - 1, 0), NKB_B - NBAND_B)
        lead = blocks_per_tile * i - band_blk
        tab = slice((2 - lead) * BLK, (2 - lead + NBAND_B) * BLK)
        far = jnp.where(blk_of_lane < band_blk, far_left,
                        jnp.where(blk_of_lane >= band_blk + NBAND_B, far_right, 0.0))
        far_hi = far.astype(BF16).astype(F32)
        far_vec = jnp.where(lane < NKB_B, far_hi, jnp.where(lane < 2 * NKB_B, far - far_hi, 0.0))
        far_q = jnp.broadcast_to(far_vec, (TQ_B, LANES)).astype(BF16)
        band = slice(band_blk * BLK, (band_blk + NBAND_B) * BLK)
        for c, qc_ref in enumerate((q0_ref, q1_ref)):
            qa = jnp.concatenate([qc_ref[i * TQ_B:(i + 1) * TQ_B, :], far_q], axis=1)
            s_ref[c] = _dot_nt(qa, ka_ref[...])
            s_ref[c, :, band] = s_ref[c, :, band] + tab_ref[:, tab]

    def softmax(s_ref, p_ref):
        for c in range(2):
            s = s_ref[c]
            p_ref[c] = jnp.exp2(s - jnp.max(s, axis=-1, keepdims=True)).astype(BF16)

    def values(i, p_ref):
        res = [_dot(p_ref[c], va_ref[...]) for c in range(2)]
        c0 = 1.0 / res[0][:, LANES:LANES + 1]
        c1 = lam / res[1][:, LANES:LANES + 1]
        o = res[0][:, :LANES] * c0 - res[1][:, :LANES] * c1
        o = _rms(o, sg_ref[...]) * (1.0 - lam_init)
        o_ref[i * TQ_B:(i + 1) * TQ_B, :] = o.astype(o_ref.dtype)

    s_bufs, p_bufs = (sa_ref, sb_ref), (pa_ref, pb_ref)
    scores(0, s_bufs[0])
    scores(1, s_bufs[1])
    softmax(s_bufs[0], p_bufs[0])
    for i in range(n_tiles):
        if i + 2 < n_tiles:
            scores(i + 2, s_bufs[i % 2])
        if i + 1 < n_tiles:
            softmax(s_bufs[(i + 1) % 2], p_bufs[(i + 1) % 2])
        values(i, p_bufs[i % 2])


def _mixb(z3, far_b, qg2, kg2, lam_vecs, subln_g, tab_b, layer):
    bsz = z3.shape[0]
    lam_init = 0.8 - 0.6 * math.exp(-0.3 * layer)
    qb0 = QB0 // LANES
    kb0 = qb0 + WB // LANES
    vb0 = VB0 // LANES
    return pl.pallas_call(
        functools.partial(_mixb_kernel, lam_init=lam_init),
        grid=(bsz, HB),
        in_specs=[
            pl.BlockSpec(memory_space=pltpu.SMEM),
            pl.BlockSpec((None, SEQ, LANES), lambda b, h: (b, 0, qb0 + h)),
            pl.BlockSpec((None, SEQ, LANES), lambda b, h: (b, 0, kb0 + h)),
            pl.BlockSpec((None, SEQ, LANES), lambda b, h: (b, 0, vb0 + h)),
            pl.BlockSpec((None, 1, LANES), lambda b, h: (layer, 0, 0)),
            pl.BlockSpec((None, 1, LANES), lambda b, h: (layer, 0, 0)),
            pl.BlockSpec((None, 4, HEAD_DIM), lambda b, h: (layer, 0, 0)),
            pl.BlockSpec((None, 1, LANES), lambda b, h: (layer, 0, 0)),
            pl.BlockSpec((None, TQ_B, (NBAND_B + 2) * BLK), lambda b, h: (h, 0, 0)),
        ],
        out_specs=pl.BlockSpec((None, SEQ, LANES), lambda b, h: (b, 0, h)),
        out_shape=jax.ShapeDtypeStruct((bsz, SEQ, WB), BF16),
        scratch_shapes=[
            pltpu.VMEM((SEQ, LANES), BF16),
            pltpu.VMEM((SEQ, LANES), BF16),
            pltpu.VMEM((SEQ, 2 * LANES), BF16),
            pltpu.VMEM((SEQ, 2 * LANES), BF16),
            pltpu.VMEM((2, TQ_B, SEQ), F32),
            pltpu.VMEM((2, TQ_B, SEQ), F32),
            pltpu.VMEM((2, TQ_B, SEQ), BF16),
            pltpu.VMEM((2, TQ_B, SEQ), BF16),
        ],
        compiler_params=pltpu.CompilerParams(
            dimension_semantics=("parallel", "arbitrary"), vmem_limit_bytes=VMEM_LIMIT),
        name=f"mixb_l{layer}",
    )(far_b, z3, z3, z3, qg2, kg2, lam_vecs, subln_g, tab_b)


def _merge_kernel(x_ref, oa_ref, ob_ref, g_ref, wg_ref, bg_ref, wa_ref, wb_ref, wo_ref, o_ref):
    x = x_ref[...]
    h = (x * g_ref[...]).astype(BF16)
    pre = _dot(h, wg_ref[...]) * _inv_rms(x) + bg_ref[...]
    gates = 1.0 / (1.0 + jnp.exp2(pre * -LOG2E))
    mix = (gates[:, :D_MODEL] * _dot(oa_ref[...], wa_ref[...])
           + gates[:, D_MODEL:] * _dot(ob_ref[...], wb_ref[...]))
    o_ref[...] = x + _dot(mix.astype(BF16), wo_ref[...])


def _merge(xf, oa, ob, ln_g, w_gate, b_gate, w_a, w_b, w_o, layer):
    n_tok = xf.shape[0]
    lsel = lambda i: (layer, 0, 0)
    return pl.pallas_call(
        _merge_kernel,
        grid=(n_tok // TM_PROJ,),
        in_specs=[
            pl.BlockSpec((TM_PROJ, D_MODEL), lambda i: (i, 0)),
            pl.BlockSpec((TM_PROJ, WA), lambda i: (i, 0)),
            pl.BlockSpec((TM_PROJ, WB), lambda i: (i, 0)),
            pl.BlockSpec((None, 1, D_MODEL), lsel),
            pl.BlockSpec((None, D_MODEL, 2 * D_MODEL), lsel),
            pl.BlockSpec((None, 1, 2 * D_MODEL), lsel),
            pl.BlockSpec((None, WA, D_MODEL), lsel),
            pl.BlockSpec((None, WB, D_MODEL), lsel),
            pl.BlockSpec((None, D_MODEL, D_MODEL), lsel),
        ],
        out_specs=pl.BlockSpec((TM_PROJ, D_MODEL), lambda i: (i, 0)),
        out_shape=jax.ShapeDtypeStruct((n_tok, D_MODEL), F32),
        compiler_params=pltpu.CompilerParams(
            dimension_semantics=("parallel",), vmem_limit_bytes=VMEM_LIMIT),
        name=f"merge_l{layer}",
    )(xf, oa, ob, ln_g, w_gate, b_gate, w_a, w_b, w_o)


def _ffn_kernel(xp_ref, xc_ref, xn_ref, g_ref, wup_ref, cw_ref, cb_ref, wdn_ref, o_ref,
                act_ref):
    tiles_per_seq = SEQ // TM_FFN
    t = pl.program_id(0) % tiles_per_seq
    g = g_ref[...]
    xc = xc_ref[...]
    hp = jnp.where(t == 0, 0.0, _rms(xp_ref[...], g))
    hn = jnp.where(t == tiles_per_seq - 1, 0.0, _rms(xn_ref[...], g))
    h2 = jnp.concatenate([hp, _rms(xc, g), hn], axis=0).astype(BF16)

    rows = TM_FFN + 2 * SUBLANES
    centre = slice(SUBLANES, SUBLANES + TM_FFN)

    def conv(col):
        u = _dot(h2, wup_ref[:, col:col + TC_FFN])
        w = cw_ref[:, col:col + TC_FFN]
        prev = pltpu.roll(u, 1, 0)[centre]
        nxt = pltpu.roll(u, rows - 1, 0)[centre]
        return ((cb_ref[:, col:col + TC_FFN] + prev * w[0:1]) + u[centre] * w[1:2]) + nxt * w[2:3]

    for c in range(D_FF // TC_FFN):
        cv, cg = c * TC_FFN, D_FF + c * TC_FFN
        val = conv(cv)
        gate = conv(cg)
        act = gate * (1.0 / (1.0 + jnp.exp2(gate * -LOG2E))) * val
        act_ref[:, cv:cv + TC_FFN] = act.astype(BF16)
    o_ref[...] = xc + _dot(act_ref[...], wdn_ref[...])


def _ffn(xf, ln_g, w_up, conv_w, conv_b, w_down, layer):
    n_tok = xf.shape[0]
    lsel = lambda i: (layer, 0, 0)
    halo_per_tile = TM_FFN // SUBLANES
    n_halo = n_tok // SUBLANES
    return pl.pallas_call(
        _ffn_kernel,
        grid=(n_tok // TM_FFN,),
        in_specs=[
            pl.BlockSpec((SUBLANES, D_MODEL), lambda i: (jnp.maximum(i * halo_per_tile - 1, 0), 0)),
            pl.BlockSpec((TM_FFN, D_MODEL), lambda i: (i, 0)),
            pl.BlockSpec((SUBLANES, D_MODEL),
                         lambda i: (jnp.minimum((i + 1) * halo_per_tile, n_halo - 1), 0)),
            pl.BlockSpec((None, 1, D_MODEL), lsel),
            pl.BlockSpec((None, D_MODEL, 2 * D_FF), lsel),
            pl.BlockSpec((None, CONV_WIDTH, 2 * D_FF), lsel),
            pl.BlockSpec((None, 1, 2 * D_FF), lsel),
            pl.BlockSpec((None, D_FF, D_MODEL), lsel),
        ],
        out_specs=pl.BlockSpec((TM_FFN, D_MODEL), lambda i: (i, 0)),
        out_shape=jax.ShapeDtypeStruct((n_tok, D_MODEL), F32),
        scratch_shapes=[
            pltpu.VMEM((TM_FFN, D_FF), BF16),
        ],
        compiler_params=pltpu.CompilerParams(
            dimension_semantics=("parallel",), vmem_limit_bytes=VMEM_LIMIT),
        name=f"ffn_l{layer}",
    )(xf, xf, xf, ln_g, w_up, conv_w, conv_b, w_down)


def _t5_bucket(rel):
    half = N_BUCKETS // 2
    max_exact = half // 2
    ret = jnp.where(rel > 0, half, 0)
    n = jnp.abs(rel)
    nf = jnp.maximum(n, 1).astype(jnp.float32)
    large = max_exact + (jnp.log(nf / max_exact) / math.log(MAX_DIST / max_exact)
                         * (half - max_exact)).astype(jnp.int32)
    large = jnp.minimum(large, half - 1)
    return ret + jnp.where(n < max_exact, n, large)


def _expand_bias(bucket, tab):
    out = jnp.zeros((tab.shape[1],) + bucket.shape, F32)
    for b in range(N_BUCKETS):
        out = out + jnp.where(bucket[None] == b, tab[b][:, None, None], 0.0)
    return out


def _bias_tables(rel_bias):
    bias_a, bias_b = rel_bias[:, :HA], rel_bias[:, HA:]
    rel_np = np.arange(3 * BLK)[None, :] - BLK - np.arange(BLK)[:, None]
    base = _expand_bias(_t5_bucket(jnp.asarray(rel_np, jnp.int32)), bias_a)
    col = np.arange(3 * BLK)[None, :]
    in_win = np.abs(rel_np) <= WINDOW
    variants = []
    for valid_cols in (col >= BLK, col >= 0, col < 2 * BLK):
        variants.append(jnp.where(jnp.asarray(in_win & valid_cols)[None], base * LOG2E, NEG))
    tab_a = jnp.stack(variants).reshape(3, KV_A, HA // KV_A * BLK, 3 * BLK)
    width = (NBAND_B + 2) * BLK
    span = TQ_B + width
    rel_1d = np.arange(span)[None, :] - (TQ_B - 1) - 2 * BLK
    diag = _expand_bias(_t5_bucket(jnp.asarray(rel_1d, jnp.int32)), bias_b)[:, 0, :] * LOG2E
    skew = jnp.tile(diag, (1, TQ_B + 1))[:, :TQ_B * (span - 1)].reshape(HB, TQ_B, span - 1)
    tab_b = skew[:, :, TQ_B - 1:TQ_B - 1 + width]
    far_b = jnp.stack([bias_b[N_BUCKETS // 2 - 1], bias_b[N_BUCKETS - 1]], axis=1) * LOG2E
    return tab_a, tab_b, far_b


def kernel(x, ln1_g, w_in, qn_a, kn_a, sink, qn_b, kn_b, lam_q1, lam_k1, lam_q2, lam_k2,
           subln_g, rel_bias, w_gate, b_gate, w_a_proj, w_b_proj, w_o, ln2_g, w_up,
           conv_w, conv_b, w_down):
    bsz, seq, d = x.shape
    assert (bsz, seq, d) == (BATCH, SEQ, D_MODEL)
    tab_a, tab_b, far_b = _bias_tables(rel_bias.astype(F32))
    row = lambda a: a.astype(F32)[:, None, :]
    twice = lambda a: jnp.concatenate([a, a], axis=-1).astype(F32)[:, None, :]
    ln1, ln2, sg = row(ln1_g), row(ln2_g), row(subln_g)
    qga, kga, qgb, kgb = twice(qn_a), twice(kn_a), twice(qn_b), twice(kn_b)
    lam_vecs = jnp.stack([lam_q1, lam_k1, lam_q2, lam_k2], axis=1).astype(F32)
    bg, cb = row(b_gate), row(conv_b)
    cw = conv_w.astype(F32)
    snk = sink.astype(F32)
    w_in_b, w_gate_b, w_a_b, w_b_b, w_o_b, w_up_b, w_down_b = (
        w.astype(BF16) for w in (w_in, w_gate, w_a_proj, w_b_proj, w_o, w_up, w_down))

    xf = x.astype(F32).reshape(bsz * seq, d)
    for layer in range(DEPTH):
        z3 = _inproj(xf, ln1, w_in_b, layer).reshape(bsz, seq, IN_W)
        oa = _mixa(z3, snk, qga, kga, tab_a, layer).reshape(bsz * seq, WA)
        ob = _mixb(z3, far_b, qgb, kgb, lam_vecs, sg, tab_b, layer).reshape(bsz * seq, WB)
        xf = _merge(xf, oa, ob, ln1, w_gate_b, bg, w_a_b, w_b_b, w_o_b, layer)
        xf = _ffn(xf, ln2, w_up_b, cw, cb, w_down_b, layer)
    return xf.reshape(bsz, seq, d).astype(x.dtype)
```
